```python
import math
import jax, jax.numpy as jnp
from jax import lax
import numpy as np

D_MODEL = 2048
BATCH = 2
SEQ = 4096
DEPTH = 1
DEC_BATCH = 1
DEC_SEQ = 16384
PAST_LEN = 128

V_HEAD_DIM = 128
ATTN_WIDTH = D_MODEL // 2
N_HEADS = ATTN_WIDTH // V_HEAD_DIM
QK_NOPE_DIM = 128
QK_ROPE_DIM = 64
Q_LORA_RANK = 512
KV_LORA_RANK = 256
ROPE_THETA = 10000.0
Q_BLOCK = 128
SSM_WIDTH = D_MODEL - ATTN_WIDTH
SSM_GROUP = 16
N_SSM_GROUPS = SSM_WIDTH // SSM_GROUP
SSM_STATE = 64
N_DIR = 2
MIX_WIDTH = ATTN_WIDTH + SSM_WIDTH
IN_PROJ_WIDTH = Q_LORA_RANK + KV_LORA_RANK + QK_ROPE_DIM + SSM_WIDTH
D_FF = int(math.ceil(8 * D_MODEL / 3 / 256)) * 256
RMS_EPS = 1e-6

kernel_name = "hymba_mla_s5_sandwich_encoder"


def rms_norm(x, g):
    xf = x.astype(jnp.float32)
    y = xf * lax.rsqrt(jnp.mean(xf * xf, axis=-1, keepdims=True) + RMS_EPS)
    return (y * g.astype(jnp.float32)).astype(x.dtype)


def rope_tables(length):
    inv = 1.0 / (ROPE_THETA ** (jnp.arange(0, QK_ROPE_DIM, 2, dtype=jnp.float32) / QK_ROPE_DIM))
    ang = jnp.arange(length, dtype=jnp.float32)[:, None] * inv[None, :]
    return jnp.cos(ang), jnp.sin(ang)


def apply_rope(x, cos, sin):
    half = QK_ROPE_DIM // 2
    xf = x.astype(jnp.float32)
    x1, x2 = xf[..., :half], xf[..., half:]
    return jnp.concatenate([x1 * cos - x2 * sin, x2 * cos + x1 * sin], axis=-1).astype(x.dtype)


def mla_mixer(c_q, c_kv, k_rope, g_q_a, w_q_b, g_kv_a, w_kv_b):
    B, L, _ = c_q.shape
    q = (rms_norm(c_q, g_q_a) @ w_q_b).reshape(B, L, N_HEADS, QK_NOPE_DIM + QK_ROPE_DIM)
    q_nope, q_rope = q[..., :QK_NOPE_DIM], q[..., QK_NOPE_DIM:]
    kv = (rms_norm(c_kv, g_kv_a) @ w_kv_b).reshape(B, L, N_HEADS, QK_NOPE_DIM + V_HEAD_DIM)
    k_nope, v = kv[..., :QK_NOPE_DIM], kv[..., QK_NOPE_DIM:]
    cos, sin = rope_tables(L)
    q_rope = apply_rope(q_rope, cos[None, :, None, :], sin[None, :, None, :])
    k_rope = apply_rope(k_rope, cos[None], sin[None])
    scale = (QK_NOPE_DIM + QK_ROPE_DIM) ** -0.5
    n_blk = L // Q_BLOCK
    qn_blocks = (q_nope * scale).reshape(B, n_blk, Q_BLOCK, N_HEADS, QK_NOPE_DIM).transpose(1, 0, 2, 3, 4)
    qr_blocks = (q_rope * scale).reshape(B, n_blk, Q_BLOCK, N_HEADS, QK_ROPE_DIM).transpose(1, 0, 2, 3, 4)

    def attend_block(args):
        qn, qr = args
        s = (jnp.einsum('bqhd,bkhd->bhqk', qn, k_nope).astype(jnp.float32)
             + jnp.einsum('bqhr,bkr->bhqk', qr, k_rope).astype(jnp.float32))
        p = jax.nn.softmax(s, axis=-1).astype(v.dtype)
        return jnp.einsum('bhqk,bkhd->bqhd', p, v)

    out = lax.map(attend_block, (qn_blocks, qr_blocks))
    return out.transpose(1, 0, 2, 3, 4).reshape(B, L, N_HEADS * V_HEAD_DIM)


def _ssm_combine(e1, e2):
    a1, b1 = e1
    a2, b2 = e2
    return a1 * a2, a2 * b1 + b2


def s5_direction(u, lam_re, lam_im, log_dt, b_re, b_im, c_re, c_im, reverse):
    f32 = jnp.float32
    dt = jnp.exp(log_dt.astype(f32))[:, None]
    lam = lax.complex(lam_re.astype(f32), lam_im.astype(f32))
    lam_bar = jnp.exp(lam * dt)
    coef = (lam_bar - 1.0) / lam
    bu = lax.complex(jnp.einsum('gpc,blgc->blgp', b_re.astype(f32), u),
                     jnp.einsum('gpc,blgc->blgp', b_im.astype(f32), u)) * coef
    a = jnp.broadcast_to(lam_bar, bu.shape)
    _, xs = lax.associative_scan(_ssm_combine, (a, bu), axis=1, reverse=reverse)
    return (jnp.einsum('gcp,blgp->blgc', c_re.astype(f32), xs.real)
            - jnp.einsum('gcp,blgp->blgc', c_im.astype(f32), xs.imag))


def s5_mixer(u, lam_re, lam_im, log_dt, b_re, b_im, c_re, c_im, d_skip, w_glu, b_glu):
    B, L, _ = u.shape
    uf = u.astype(jnp.float32).reshape(B, L, N_SSM_GROUPS, SSM_GROUP)
    y = (s5_direction(uf, lam_re[0], lam_im[0], log_dt[0], b_re[0], b_im[0], c_re[0], c_im[0], False)
         + s5_direction(uf, lam_re[1], lam_im[1], log_dt[1], b_re[1], b_im[1], c_re[1], c_im[1], True)
         + d_skip.astype(jnp.float32) * uf)
    y = jax.nn.gelu(y.reshape(B, L, SSM_WIDTH)).astype(u.dtype)
    gate = jax.nn.sigmoid((y @ w_glu + b_glu).astype(jnp.float32))
    return (y.astype(jnp.float32) * gate).astype(u.dtype)


def encoder_layer(x, g_pre_mix, w_in, g_q_a, w_q_b, g_kv_a, w_kv_b,
                  ssm_lam_re, ssm_lam_im, ssm_log_dt, ssm_b_re, ssm_b_im, ssm_c_re, ssm_c_im,
                  ssm_d, w_glu, b_glu, g_out_attn, g_out_ssm, w_out, g_post_mix,
                  g_pre_ffn, w_gate, w_up, w_down, g_post_ffn):
    h = rms_norm(x, g_pre_mix)
    z = h @ w_in
    i0 = Q_LORA_RANK
    i1 = i0 + KV_LORA_RANK
    i2 = i1 + QK_ROPE_DIM
    c_q, c_kv, k_rope, u = z[..., :i0], z[..., i0:i1], z[..., i1:i2], z[..., i2:]
    a = mla_mixer(c_q, c_kv, k_rope, g_q_a, w_q_b, g_kv_a, w_kv_b)
    s = s5_mixer(u, ssm_lam_re, ssm_lam_im, ssm_log_dt, ssm_b_re, ssm_b_im, ssm_c_re, ssm_c_im,
                 ssm_d, w_glu, b_glu)
    m = jnp.concatenate([rms_norm(a, g_out_attn), rms_norm(s.astype(x.dtype), g_out_ssm)], axis=-1) @ w_out
    x = x + rms_norm(m, g_post_mix)
    h = rms_norm(x, g_pre_ffn)
    f = (jax.nn.silu(h @ w_gate) * (h @ w_up)) @ w_down
    return x + rms_norm(f, g_post_ffn)


def trunk(x, weights):
    for l in range(DEPTH):
        x = encoder_layer(x, *[w[l] for w in weights])
    return x


def setup_inputs(seed: int = 0) -> dict:
    key = jax.random.key(seed)
    ks = jax.random.split(key, 32)
    f32 = jnp.float32

    def nrm(k, shape, scale):
        return jax.random.normal(k, shape, f32) * scale

    def gain(k, shape):
        return 1.0 + 0.02 * jax.random.normal(k, shape, f32)

    G, P, C = N_SSM_GROUPS, SSM_STATE, SSM_GROUP
    n_idx = jnp.arange(P, dtype=f32)
    lam_re = -0.5 * jnp.exp(0.01 * jax.random.normal(ks[8], (DEPTH, N_DIR, G, P), f32))
    lam_im = math.pi * n_idx + 0.01 * jax.random.normal(ks[9], (DEPTH, N_DIR, G, P), f32)
    log_dt = jax.random.uniform(ks[10], (DEPTH, N_DIR, G), f32, math.log(1e-3), math.log(1e-1))
    return {
        "x_prompt": jax.random.normal(ks[0], (BATCH, SEQ, D_MODEL), f32),
        "x_sample": jax.random.normal(ks[1], (DEC_BATCH, DEC_SEQ, D_MODEL), f32),
        "g_pre_mix": gain(ks[2], (DEPTH, D_MODEL)),
        "w_in": nrm(ks[3], (DEPTH, D_MODEL, IN_PROJ_WIDTH), D_MODEL ** -0.5),
        "g_q_a": gain(ks[4], (DEPTH, Q_LORA_RANK)),
        "w_q_b": nrm(ks[5], (DEPTH, Q_LORA_RANK, N_HEADS * (QK_NOPE_DIM + QK_ROPE_DIM)), Q_LORA_RANK ** -0.5),
        "g_kv_a": gain(ks[6], (DEPTH, KV_LORA_RANK)),
        "w_kv_b": nrm(ks[7], (DEPTH, KV_LORA_RANK, N_HEADS * (QK_NOPE_DIM + V_HEAD_DIM)), KV_LORA_RANK ** -0.5),
        "ssm_lam_re": lam_re,
        "ssm_lam_im": lam_im,
        "ssm_log_dt": log_dt,
        "ssm_b_re": nrm(ks[11], (DEPTH, N_DIR, G, P, C), (2.0 * C) ** -0.5),
        "ssm_b_im": nrm(ks[12], (DEPTH, N_DIR, G, P, C), (2.0 * C) ** -0.5),
        "ssm_c_re": nrm(ks[13], (DEPTH, N_DIR, G, C, P), (2.0 * P) ** -0.5),
        "ssm_c_im": nrm(ks[14], (DEPTH, N_DIR, G, C, P), (2.0 * P) ** -0.5),
        "ssm_d": nrm(ks[15], (DEPTH, G, C), 1.0),
        "w_glu": nrm(ks[16], (DEPTH, SSM_WIDTH, SSM_WIDTH), SSM_WIDTH ** -0.5),
        "b_glu": nrm(ks[17], (DEPTH, SSM_WIDTH), 0.01),
        "g_out_attn": gain(ks[18], (DEPTH, ATTN_WIDTH)),
        "g_out_ssm": gain(ks[19], (DEPTH, SSM_WIDTH)),
        "w_out": nrm(ks[20], (DEPTH, MIX_WIDTH, D_MODEL), MIX_WIDTH ** -0.5),
        "g_post_mix": gain(ks[21], (DEPTH, D_MODEL)),
        "g_pre_ffn": gain(ks[22], (DEPTH, D_MODEL)),
        "w_gate": nrm(ks[23], (DEPTH, D_MODEL, D_FF), D_MODEL ** -0.5),
        "w_up": nrm(ks[24], (DEPTH, D_MODEL, D_FF), D_MODEL ** -0.5),
        "w_down": nrm(ks[25], (DEPTH, D_FF, D_MODEL), D_FF ** -0.5),
        "g_post_ffn": gain(ks[26], (DEPTH, D_MODEL)),
    }


def reference(x_prompt, x_sample, g_pre_mix, w_in, g_q_a, w_q_b, g_kv_a, w_kv_b,
              ssm_lam_re, ssm_lam_im, ssm_log_dt, ssm_b_re, ssm_b_im, ssm_c_re, ssm_c_im,
              ssm_d, w_glu, b_glu, g_out_attn, g_out_ssm, w_out, g_post_mix,
              g_pre_ffn, w_gate, w_up, w_down, g_post_ffn):
    weights = (g_pre_mix, w_in, g_q_a, w_q_b, g_kv_a, w_kv_b,
               ssm_lam_re, ssm_lam_im, ssm_log_dt, ssm_b_re, ssm_b_im, ssm_c_re, ssm_c_im,
               ssm_d, w_glu, b_glu, g_out_attn, g_out_ssm, w_out, g_post_mix,
               g_pre_ffn, w_gate, w_up, w_down, g_post_ffn)
    y_prompt = trunk(x_prompt, weights)
    y_sample = trunk(x_sample, weights)
    return (y_prompt, y_sample)
```

```python
import functools
import math

import jax
import jax.numpy as jnp
from jax import lax
from jax.experimental import pallas as pl
from jax.experimental.pallas import tpu as pltpu

F32 = jnp.float32
BF16 = jnp.bfloat16

RMS_EPS = 1e-6
ROPE_THETA = 10000.0
LANES = 128
SUBLANES = 8
VMEM_LIMIT_BYTES = 56 * 1024 * 1024

N_HEADS = 8
QK_NOPE = 128
QK_ROPE = 64
V_HEAD = 128
HEAD_PAD = 256
Q_LORA = 512
KV_LORA = 256
SSM_GROUP = 16
SSM_STATE = 64
SSM_CHUNK = LANES // SSM_GROUP
SSM_PITCH_PAD = 8


def _rms(x, g):
    return x * lax.rsqrt(jnp.mean(x * x, axis=-1, keepdims=True) + RMS_EPS) * g


def _params(sem):
    return pltpu.CompilerParams(dimension_semantics=sem, vmem_limit_bytes=VMEM_LIMIT_BYTES)


def _tile(n, target):
    t = min(n, target)
    while n % t:
        t //= 2
    return t


def _const_spec(shape):
    nd = len(shape)
    return pl.BlockSpec(shape, lambda *_: (0,) * nd, pipeline_mode=pl.Buffered(1))


def _in_proj_kernel(x_ref, gpre_ref, win_ref, gq_ref, wq_ref, gkv_ref, wk_ref, wv_ref, cs_ref,
                    q_ref, k_ref, v_ref, u_ref, *, q_scale):
    x = x_ref[...]
    h = _rms(x, gpre_ref[...])
    z = jnp.dot(h.astype(BF16), win_ref[...], preferred_element_type=F32)
    c_q = z[:, :Q_LORA]
    c_kv = z[:, Q_LORA:Q_LORA + KV_LORA]
    n_u = u_ref.shape[1]
    u_ref[...] = z[:, Q_LORA + KV_LORA:Q_LORA + KV_LORA + n_u]
    k_rope = z[:, Q_LORA + KV_LORA + n_u:]
    cs = cs_ref[...]

    def rotate(t):
        t = t * cs
        return t + pltpu.roll(t, QK_ROPE, 1)

    cqn = _rms(c_q, gq_ref[...]).astype(BF16)
    q = jnp.dot(cqn, wq_ref[...], preferred_element_type=F32) * q_scale
    for hd in range(N_HEADS):
        lo = hd * HEAD_PAD
        q_ref[:, lo:lo + QK_NOPE] = q[:, lo:lo + QK_NOPE].astype(BF16)
        q_ref[:, lo + QK_NOPE:lo + HEAD_PAD] = rotate(q[:, lo + QK_NOPE:lo + HEAD_PAD]).astype(BF16)

    ckvn = _rms(c_kv, gkv_ref[...]).astype(BF16)
    k_nope = jnp.dot(ckvn, wk_ref[...], preferred_element_type=F32)
    v_ref[...] = jnp.dot(ckvn, wv_ref[...], preferred_element_type=F32).astype(BF16)
    lane = lax.broadcasted_iota(jnp.int32, k_rope.shape, 1)
    kr = jnp.where(lane < QK_ROPE, rotate(k_rope), 0.0).astype(BF16)
    for hd in range(N_HEADS):
        lo = hd * HEAD_PAD
        k_ref[:, lo:lo + QK_NOPE] = k_nope[:, hd * QK_NOPE:(hd + 1) * QK_NOPE].astype(BF16)
        k_ref[:, lo + QK_NOPE:lo + HEAD_PAD] = kr


def _in_proj(x2d, seq_len, w, cs):
    t, d = x2d.shape
    tm = _tile(seq_len, 512)
    n_u = w["n_u"]
    kern = functools.partial(_in_proj_kernel, q_scale=w["q_scale"])
    tiles_per_seq = seq_len // tm
    row = lambda i: (i, 0)
    return pl.pallas_call(
        kern,
        grid=(t // tm,),
        in_specs=[
            pl.BlockSpec((tm, d), row),
            _const_spec((1, d)),
            _const_spec(w["w_in"].shape),
            _const_spec((1, Q_LORA)),
            _const_spec(w["w_q"].shape),
            _const_spec((1, KV_LORA)),
            _const_spec(w["w_k"].shape),
            _const_spec(w["w_v"].shape),
            pl.BlockSpec((tm, LANES), lambda i: (i % tiles_per_seq, 0)),
        ],
        out_specs=[
            pl.BlockSpec((tm, N_HEADS * HEAD_PAD), row),
            pl.BlockSpec((tm, N_HEADS * HEAD_PAD), row),
            pl.BlockSpec((tm, N_HEADS * V_HEAD), row),
            pl.BlockSpec((tm, n_u), row),
        ],
        out_shape=[
            jax.ShapeDtypeStruct((t, N_HEADS * HEAD_PAD), BF16),
            jax.ShapeDtypeStruct((t, N_HEADS * HEAD_PAD), BF16),
            jax.ShapeDtypeStruct((t, N_HEADS * V_HEAD), BF16),
            jax.ShapeDtypeStruct((t, n_u), F32),
        ],
        compiler_params=_params(("arbitrary",)),
        name="in_proj",
    )(x2d, w["g_pre_mix"], w["w_in"], w["g_q_a"], w["w_q"], w["g_kv_a"], w["w_k"], w["w_v"], cs)


def _flash_kernel(q_ref, k_ref, v_ref, o_ref, *, tk):
    q = q_ref[0]
    tq = q.shape[0]
    n_k = k_ref.shape[1] // tk

    def body(i, carry):
        m, l, acc = carry
        start = pl.multiple_of(i * tk, tk)
        ks = k_ref[0, pl.ds(start, tk), :]
        vs = v_ref[0, pl.ds(start, tk), :]
        s = lax.dot_general(q, ks, (((1,), (1,)), ((), ())), preferred_element_type=F32)
        m_new = jnp.maximum(m, jnp.max(s, axis=1, keepdims=True))
        alpha = jnp.exp2(m - m_new)
        p = jnp.exp2(s - m_new)
        l = alpha * l + jnp.sum(p, axis=1, keepdims=True)
        acc = alpha * acc + jnp.dot(p.astype(BF16), vs, preferred_element_type=F32)
        return m_new, l, acc

    init = (jnp.full((tq, 1), -jnp.inf, F32), jnp.zeros((tq, 1), F32), jnp.zeros((tq, V_HEAD), F32))
    _, l, acc = lax.fori_loop(0, n_k, body, init)
    o_ref[0] = (acc / l).astype(o_ref.dtype)


def _attention(q, k, v, batch, seq_len):
    q = q.reshape(batch, seq_len, N_HEADS * HEAD_PAD)
    k = k.reshape(batch, seq_len, N_HEADS * HEAD_PAD)
    v = v.reshape(batch, seq_len, N_HEADS * V_HEAD)
    tq = _tile(seq_len, 256)
    tk = _tile(seq_len, 512)
    out = pl.pallas_call(
        functools.partial(_flash_kernel, tk=tk),
        grid=(batch, N_HEADS, seq_len // tq),
        in_specs=[
            pl.BlockSpec((1, tq, HEAD_PAD), lambda b, h, i: (b, i, h)),
            pl.BlockSpec((1, seq_len, HEAD_PAD), lambda b, h, i: (b, 0, h)),
            pl.BlockSpec((1, seq_len, V_HEAD), lambda b, h, i: (b, 0, h)),
        ],
        out_specs=pl.BlockSpec((1, tq, V_HEAD), lambda b, h, i: (b, i, h)),
        out_shape=jax.ShapeDtypeStruct((batch, seq_len, N_HEADS * V_HEAD), BF16),
        compiler_params=_params(("arbitrary", "arbitrary", "arbitrary")),
        name="attention",
    )(q, k, v)
    return out.reshape(batch * seq_len, N_HEADS * V_HEAD)


def _ssm_matrices(lam_re, lam_im, log_dt, b_re, b_im, c_re, c_im):
    t_c = SSM_CHUNK
    n_g = lam_re.shape[1]
    w_in, w_state, a_r, a_s = [], [], [], []
    toep = 0.0
    for d in range(2):
        dt = jnp.exp(log_dt[d].astype(F32))[:, None]
        lam = lax.complex(lam_re[d].astype(F32), lam_im[d].astype(F32))
        a = jnp.exp(lam * dt)
        coef = (a - 1.0) / lam
        bt = lax.complex(b_re[d].astype(F32), b_im[d].astype(F32)) * coef[:, :, None]
        cc = lax.complex(c_re[d].astype(F32), c_im[d].astype(F32))
        steps = jnp.arange(t_c + 1, dtype=F32)
        apow = jnp.exp(lam[None] * dt[None] * steps[:, None, None])
        in_pow = apow[:t_c][::-1] if d == 0 else apow[:t_c]
        win = jnp.einsum("sgp,gpc->gscp", in_pow, bt).reshape(n_g, LANES, SSM_STATE)
        w_in.append(jnp.concatenate([win.real, win.imag, win.imag, win.real], axis=-1))
        out_pow = apow[1:] if d == 0 else apow[1:][::-1]
        wst = jnp.einsum("gcp,tgp->gptc", cc, out_pow).reshape(n_g, SSM_STATE, LANES)
        w_state.append(jnp.concatenate([wst.real, -wst.imag], axis=1))
        kern = jnp.einsum("gcp,kgp,gpe->kgce", cc, apow[:t_c], bt).real
        s_idx = jnp.arange(t_c)[:, None]
        t_idx = jnp.arange(t_c)[None, :]
        lag = (t_idx - s_idx) if d == 0 else (s_idx - t_idx)
        blocks = jnp.where((lag >= 0)[:, :, None, None, None], kern[jnp.clip(lag, 0, t_c - 1)], 0.0)
        toep = toep + blocks.transpose(2, 0, 4, 1, 3).reshape(n_g, LANES, LANES)
        a_t = apow[t_c]
        a_r.append(jnp.concatenate([a_t.real, a_t.real], axis=-1))
        a_s.append(jnp.concatenate([-a_t.imag, a_t.imag], axis=-1))
    w_out = jnp.concatenate([toep, w_state[0], w_state[1]], axis=1)
    return dict(w_in_f=w_in[0].astype(BF16), w_in_b=w_in[1].astype(BF16), w_out=w_out.astype(BF16),
                ar_f=a_r[0], as_f=a_s[0], ar_b=a_r[1], as_b=a_s[1])


def _ssm_state_kernel(uf_ref, ub_ref, wf_ref, wb_ref, arf_ref, asf_ref, arb_ref, asb_ref,
                      xf_ref, xb_ref, vf, vft, vb, vbt, carry, *, tj, pitch):
    n_g = uf_ref.shape[1]

    @pl.when(pl.program_id(1) == 0)
    def _():
        carry[...] = jnp.zeros_like(carry)

    def project(g, _):
        row = pl.multiple_of(g * pitch, SUBLANES)
        pf = jnp.dot(uf_ref[0, g], wf_ref[g], preferred_element_type=F32)
        vf[pl.ds(row, tj), :] = pf[:, :LANES]
        vft[pl.ds(row, tj), :] = pf[:, LANES:]
        pb = jnp.dot(ub_ref[0, g], wb_ref[g], preferred_element_type=F32)
        vb[pl.ds(row, tj), :] = pb[:, :LANES]
        vbt[pl.ds(row, tj), :] = pb[:, LANES:]
        return 0

    lax.fori_loop(0, n_g, project, 0)

    n_blk = n_g // SUBLANES

    def scan(j, state):
        jb = tj - 1 - j
        new = []
        for blk in range(n_blk):
            base = blk * SUBLANES * pitch
            gs = pl.ds(blk * SUBLANES, SUBLANES)
            for d, (v, vt, ar_ref, as_ref, jj) in enumerate(
                    ((vf, vft, arf_ref, asf_ref, j), (vb, vbt, arb_ref, asb_ref, jb))):
                x, xt = state[(blk * 2 + d) * 2], state[(blk * 2 + d) * 2 + 1]
                rows = pl.ds(base + jj, SUBLANES, stride=pitch)
                vx = v[rows, :]
                vxt = vt[rows, :]
                v[rows, :] = x
                ar = ar_ref[gs, :]
                a_s = as_ref[gs, :]
                new.append(ar * x + a_s * xt + vx)
                new.append(ar * xt - a_s * x + vxt)
        return tuple(new)

    init = tuple(carry[i] for i in range(4 * n_blk))
    final = lax.fori_loop(0, tj, scan, init)
    for i in range(4 * n_blk):
        carry[i] = final[i]

    def emit(g, _):
        row = pl.multiple_of(g * pitch, SUBLANES)
        xf_ref[0, g] = vf[pl.ds(row, tj), :].astype(BF16)
        xb_ref[0, g] = vb[pl.ds(row, tj), :].astype(BF16)
        return 0

    lax.fori_loop(0, n_g, emit, 0)


def _ssm_out_kernel(u_ref, xf_ref, xb_ref, w_ref, y_ref):
    n_g = u_ref.shape[1]

    def body(g, _):
        lhs = jnp.concatenate([u_ref[0, g], xf_ref[0, g], xb_ref[0, g]], axis=1)
        y_ref[0, g] = jnp.dot(lhs, w_ref[g], preferred_element_type=F32)
        return 0

    lax.fori_loop(0, n_g, body, 0)


def _ssm(u2d, batch, seq_len, w):
    n_g = u2d.shape[1] // SSM_GROUP
    n_j = seq_len // SSM_CHUNK
    uc = (u2d.astype(BF16).reshape(batch, n_j, SSM_CHUNK, n_g, SSM_GROUP)
          .transpose(0, 3, 1, 2, 4).reshape(batch, n_g, n_j, LANES))
    tj = _tile(n_j, 128)
    n_t = n_j // tj
    pitch = tj + SSM_PITCH_PAD
    blk = (1, n_g, tj, LANES)
    fwd = lambda b, i: (b, 0, i, 0)
    bwd = lambda b, i: (b, 0, n_t - 1 - i, 0)
    scan_buf = pltpu.VMEM((n_g * pitch, LANES), F32)
    xf, xb = pl.pallas_call(
        functools.partial(_ssm_state_kernel, tj=tj, pitch=pitch),
        grid=(batch, n_t),
        in_specs=[
            pl.BlockSpec(blk, fwd),
            pl.BlockSpec(blk, bwd),
            _const_spec(w["w_in_f"].shape),
            _const_spec(w["w_in_b"].shape),
            _const_spec(w["ar_f"].shape),
            _const_spec(w["as_f"].shape),
            _const_spec(w["ar_b"].shape),
            _const_spec(w["as_b"].shape),
        ],
        out_specs=[pl.BlockSpec(blk, fwd), pl.BlockSpec(blk, bwd)],
        out_shape=[jax.ShapeDtypeStruct(uc.shape, BF16)] * 2,
        scratch_shapes=[scan_buf, scan_buf, scan_buf, scan_buf,
                        pltpu.VMEM((4 * (n_g // SUBLANES), SUBLANES, LANES), F32)],
        compiler_params=_params(("arbitrary", "arbitrary")),
        name="ssm_state",
    )(uc, uc, w["w_in_f"], w["w_in_b"], w["ar_f"], w["as_f"], w["ar_b"], w["as_b"])
    same = lambda b, i: (b, 0, i, 0)
    yc = pl.pallas_call(
        _ssm_out_kernel,
        grid=(batch, n_t),
        in_specs=[pl.BlockSpec(blk, same), pl.BlockSpec(blk, same), pl.BlockSpec(blk, same),
                  _const_spec(w["w_out"].shape)],
        out_specs=pl.BlockSpec(blk, same),
        out_shape=jax.ShapeDtypeStruct(uc.shape, F32),
        compiler_params=_params(("arbitrary", "arbitrary")),
        name="ssm_out",
    )(uc, xf, xb, w["w_out"])
    return (yc.reshape(batch, n_g, n_j, SSM_CHUNK, SSM_GROUP)
            .transpose(0, 2, 3, 1, 4).reshape(batch * seq_len, n_g * SSM_GROUP))


def _mix_out_kernel(ys_ref, u_ref, a_ref, x_ref, d_ref, wglu_ref, bglu_ref, ga_ref, gs_ref,
                    woa_ref, wos_ref, gpost_ref, gffn_ref, x1_ref, h2_ref):
    y = jax.nn.gelu(ys_ref[...] + d_ref[...] * u_ref[...])
    gate = jax.nn.sigmoid(jnp.dot(y.astype(BF16), wglu_ref[...], preferred_element_type=F32)
                          + bglu_ref[...])
    s = y * gate
    an = _rms(a_ref[...].astype(F32), ga_ref[...]).astype(BF16)
    sn = _rms(s, gs_ref[...]).astype(BF16)
    m = (jnp.dot(an, woa_ref[...], preferred_element_type=F32)
         + jnp.dot(sn, wos_ref[...], preferred_element_type=F32))
    x1 = x_ref[...] + _rms(m, gpost_ref[...])
    x1_ref[...] = x1
    h2_ref[...] = _rms(x1, gffn_ref[...]).astype(BF16)


def _mix_out(ys, u, a, x2d, w):
    t, d = x2d.shape
    n_a = a.shape[1]
    n_s = ys.shape[1]
    tm = _tile(t, 256)
    row = lambda i: (i, 0)
    return pl.pallas_call(
        _mix_out_kernel,
        grid=(t // tm,),
        in_specs=[
            pl.BlockSpec((tm, n_s), row),
            pl.BlockSpec((tm, n_s), row),
            pl.BlockSpec((tm, n_a), row),
            pl.BlockSpec((tm, d), row),
            _const_spec((1, n_s)),
            _const_spec(w["w_glu"].shape),
            _const_spec((1, n_s)),
            _const_spec((1, n_a)),
            _const_spec((1, n_s)),
            _const_spec(w["w_out_a"].shape),
            _const_spec(w["w_out_s"].shape),
            _const_spec((1, d)),
            _const_spec((1, d)),
        ],
        out_specs=[pl.BlockSpec((tm, d), row), pl.BlockSpec((tm, d), row)],
        out_shape=[jax.ShapeDtypeStruct((t, d), F32), jax.ShapeDtypeStruct((t, d), BF16)],
        compiler_params=_params(("arbitrary",)),
        name="mix_out",
    )(ys, u, a, x2d, w["ssm_d"], w["w_glu"], w["b_glu"], w["g_out_attn"], w["g_out_ssm"],
      w["w_out_a"], w["w_out_s"], w["g_post_mix"], w["g_pre_ffn"])


def _ffn_kernel(h_ref, x1_ref, wg_ref, wu_ref, wd_ref, gpost_ref, o_ref, acc_ref):
    j = pl.program_id(1)

    @pl.when(j == 0)
    def _():
        acc_ref[...] = jnp.zeros_like(acc_ref)

    h = h_ref[...]
    gate = jnp.dot(h, wg_ref[...], preferred_element_type=F32)
    up = jnp.dot(h, wu_ref[...], preferred_element_type=F32)
    act = (jax.nn.silu(gate) * up).astype(BF16)
    acc_ref[...] += jnp.dot(act, wd_ref[...], preferred_element_type=F32)

    @pl.when(j == pl.num_programs(1) - 1)
    def _():
        o_ref[...] = x1_ref[...] + _rms(acc_ref[...], gpost_ref[...])


def _ffn(h2, x1, w):
    t, d = x1.shape
    d_ff = w["w_gate"].shape[1]
    tm = _tile(t, 512)
    tf = _tile(d_ff, 512)
    return pl.pallas_call(
        _ffn_kernel,
        grid=(t // tm, d_ff // tf),
        in_specs=[
            pl.BlockSpec((tm, d), lambda i, j: (i, 0)),
            pl.BlockSpec((tm, d), lambda i, j: (i, 0)),
            pl.BlockSpec((d, tf), lambda i, j: (0, j)),
            pl.BlockSpec((d, tf), lambda i, j: (0, j)),
            pl.BlockSpec((tf, d), lambda i, j: (j, 0)),
            pl.BlockSpec((1, d), lambda i, j: (0, 0)),
        ],
        out_specs=pl.BlockSpec((tm, d), lambda i, j: (i, 0)),
        out_shape=jax.ShapeDtypeStruct((t, d), F32),
        scratch_shapes=[pltpu.VMEM((tm, d), F32)],
        compiler_params=_params(("arbitrary", "arbitrary")),
        name="ffn",
    )(h2, x1, w["w_gate"], w["w_up"], w["w_down"], w["g_post_ffn"])


def _rope_table(length):
    inv = 1.0 / (ROPE_THETA ** (jnp.arange(0, QK_ROPE, 2, dtype=F32) / QK_ROPE))
    ang = jnp.arange(length, dtype=F32)[:, None] * inv[None, :]
    cos, sin = jnp.cos(ang), jnp.sin(ang)
    return jnp.concatenate([cos, cos, -sin, sin], axis=-1)


def _swap_halves(wr):
    half = QK_ROPE // 2
    return jnp.concatenate([wr[..., half:], wr[..., :half]], axis=-1)


def _prepare_weights(g_pre_mix, w_in, g_q_a, w_q_b, g_kv_a, w_kv_b, lam_re, lam_im, log_dt,
                     b_re, b_im, c_re, c_im, ssm_d, w_glu, b_glu, g_out_attn, g_out_ssm, w_out,
                     g_post_mix, g_pre_ffn, w_gate, w_up, w_down, g_post_ffn):
    row = lambda g: g.astype(F32).reshape(1, -1)
    i1 = Q_LORA + KV_LORA
    i2 = i1 + QK_ROPE
    w_rope = w_in[:, i1:i2]
    w_in_ext = jnp.concatenate([w_in[:, :i1], w_in[:, i2:], w_rope, _swap_halves(w_rope)], axis=1)
    wq = w_q_b.reshape(Q_LORA, N_HEADS, QK_NOPE + QK_ROPE)
    wq_ext = jnp.concatenate([wq, _swap_halves(wq[..., QK_NOPE:])], axis=-1)
    wkv = w_kv_b.reshape(KV_LORA, N_HEADS, QK_NOPE + V_HEAD)
    n_a = N_HEADS * V_HEAD
    out = dict(
        n_u=w_in.shape[1] - i2,
        q_scale=float((QK_NOPE + QK_ROPE) ** -0.5 * math.log2(math.e)),
        g_pre_mix=row(g_pre_mix), w_in=w_in_ext.astype(BF16),
        g_q_a=row(g_q_a), w_q=wq_ext.reshape(Q_LORA, N_HEADS * HEAD_PAD).astype(BF16),
        g_kv_a=row(g_kv_a),
        w_k=wkv[..., :QK_NOPE].reshape(KV_LORA, N_HEADS * QK_NOPE).astype(BF16),
        w_v=wkv[..., QK_NOPE:].reshape(KV_LORA, N_HEADS * V_HEAD).astype(BF16),
        ssm_d=row(ssm_d), w_glu=w_glu.astype(BF16), b_glu=row(b_glu),
        g_out_attn=row(g_out_attn), g_out_ssm=row(g_out_ssm),
        w_out_a=w_out[:n_a].astype(BF16), w_out_s=w_out[n_a:].astype(BF16),
        g_post_mix=row(g_post_mix), g_pre_ffn=row(g_pre_ffn),
        w_gate=w_gate.astype(BF16), w_up=w_up.astype(BF16), w_down=w_down.astype(BF16),
        g_post_ffn=row(g_post_ffn),
    )
    out.update(_ssm_matrices(lam_re, lam_im, log_dt, b_re, b_im, c_re, c_im))
    return out


def _layer(x, w):
    batch, seq_len, d = x.shape
    x2d = x.reshape(batch * seq_len, d)
    q, k, v, u = _in_proj(x2d, seq_len, w, _rope_table(seq_len))
    a = _attention(q, k, v, batch, seq_len)
    ys = _ssm(u, batch, seq_len, w)
    x1, h2 = _mix_out(ys, u, a, x2d, w)
    return _ffn(h2, x1, w).reshape(batch, seq_len, d)


def kernel(x_prompt, x_sample, g_pre_mix, w_in, g_q_a, w_q_b, g_kv_a, w_kv_b, ssm_lam_re, ssm_lam_im, ssm_log_dt, ssm_b_re, ssm_b_im, ssm_c_re, ssm_c_im, ssm_d, w_glu, b_glu, g_out_attn, g_out_ssm, w_out, g_post_mix, g_pre_ffn, w_gate, w_up, w_down, g_post_ffn):
    weights = (g_pre_mix, w_in, g_q_a, w_q_b, g_kv_a, w_kv_b, ssm_lam_re, ssm_lam_im, ssm_log_dt,
               ssm_b_re, ssm_b_im, ssm_c_re, ssm_c_im, ssm_d, w_glu, b_glu, g_out_attn, g_out_ssm,
               w_out, g_post_mix, g_pre_ffn, w_gate, w_up, w_down, g_post_ffn)
    depth = g_pre_mix.shape[0]
    for layer in range(depth):
        w = _prepare_weights(*[p[layer] for p in weights])
        x_prompt = _layer(x_prompt, w)
        x_sample = _layer(x_sample, w)
    return (x_prompt, x_sample)
```

```python
import functools
import math

import jax
import jax.numpy as jnp
from jax import lax
from jax.experimental import pallas as pl
from jax.experimental.pallas import tpu as pltpu

F32 = jnp.float32
BF16 = jnp.bfloat16

RMS_EPS = 1e-6
ROPE_THETA = 10000.0
LANES = 128
SUBLANES = 8
VMEM_LIMIT_BYTES = 56 * 1024 * 1024

N_HEADS = 8
QK_NOPE = 128
QK_ROPE = 64
V_HEAD = 128
HEAD_PAD = 256
Q_LORA = 512
KV_LORA = 256
SSM_GROUP = 16
SSM_STATE = 64
SSM_CHUNK = LANES // SSM_GROUP
SSM_PITCH_PAD = 8


def _rms(x, g):
    return x * lax.rsqrt(jnp.mean(x * x, axis=-1, keepdims=True) + RMS_EPS) * g


def _params(sem):
    return pltpu.CompilerParams(dimension_semantics=sem, vmem_limit_bytes=VMEM_LIMIT_BYTES)


def _tile(n, target):
    t = min(n, target)
    while n % t:
        t //= 2
    return t


def _const_spec(shape):
    nd = len(shape)
    return pl.BlockSpec(shape, lambda *_: (0,) * nd, pipeline_mode=pl.Buffered(1))


def _in_proj_kernel(x_ref, gpre_ref, win_ref, gq_ref, wq_ref, gkv_ref, wk_ref, wvt_ref, cs_ref,
                    q_ref, k_ref, vt_ref, u_ref, *, q_scale):
    x = x_ref[...]
    h = _rms(x, gpre_ref[...])
    z = jnp.dot(h.astype(BF16), win_ref[...], preferred_element_type=F32)
    c_q = z[:, :Q_LORA]
    c_kv = z[:, Q_LORA:Q_LORA + KV_LORA]
    n_u = u_ref.shape[1]
    u_ref[...] = z[:, Q_LORA + KV_LORA:Q_LORA + KV_LORA + n_u]
    k_rope = z[:, Q_LORA + KV_LORA + n_u:]
    cs = cs_ref[...]

    def rotate(t):
        t = t * cs
        return t + pltpu.roll(t, QK_ROPE, 1)

    cqn = _rms(c_q, gq_ref[...]).astype(BF16)
    q = jnp.dot(cqn, wq_ref[...], preferred_element_type=F32) * q_scale
    for hd in range(N_HEADS):
        lo = hd * HEAD_PAD
        q_ref[:, lo:lo + QK_NOPE] = q[:, lo:lo + QK_NOPE].astype(BF16)
        q_ref[:, lo + QK_NOPE:lo + HEAD_PAD] = rotate(q[:, lo + QK_NOPE:lo + HEAD_PAD]).astype(BF16)

    ckvn = _rms(c_kv, gkv_ref[...]).astype(BF16)
    k_nope = jnp.dot(ckvn, wk_ref[...], preferred_element_type=F32)
    vt_ref[...] = lax.dot_general(wvt_ref[...], ckvn, (((1,), (1,)), ((), ())),
                                  preferred_element_type=F32).astype(BF16)
    lane = lax.broadcasted_iota(jnp.int32, k_rope.shape, 1)
    kr = jnp.where(lane < QK_ROPE, rotate(k_rope), 0.0).astype(BF16)
    for hd in range(N_HEADS):
        lo = hd * HEAD_PAD
        k_ref[:, lo:lo + QK_NOPE] = k_nope[:, hd * QK_NOPE:(hd + 1) * QK_NOPE].astype(BF16)
        k_ref[:, lo + QK_NOPE:lo + HEAD_PAD] = kr


def _in_proj(x2d, seq_len, w, cs):
    t, d = x2d.shape
    tm = _tile(seq_len, 512)
    n_u = w["n_u"]
    kern = functools.partial(_in_proj_kernel, q_scale=w["q_scale"])
    tiles_per_seq = seq_len // tm
    row = lambda i: (i, 0)
    return pl.pallas_call(
        kern,
        grid=(t // tm,),
        in_specs=[
            pl.BlockSpec((tm, d), row),
            _const_spec((1, d)),
            _const_spec(w["w_in"].shape),
            _const_spec((1, Q_LORA)),
            _const_spec(w["w_q"].shape),
            _const_spec((1, KV_LORA)),
            _const_spec(w["w_k"].shape),
            _const_spec(w["w_vt"].shape),
            pl.BlockSpec((tm, LANES), lambda i: (i % tiles_per_seq, 0)),
        ],
        out_specs=[
            pl.BlockSpec((tm, N_HEADS * HEAD_PAD), row),
            pl.BlockSpec((tm, N_HEADS * HEAD_PAD), row),
            pl.BlockSpec((N_HEADS * V_HEAD, tm), lambda i: (0, i)),
            pl.BlockSpec((tm, n_u), row),
        ],
        out_shape=[
            jax.ShapeDtypeStruct((t, N_HEADS * HEAD_PAD), BF16),
            jax.ShapeDtypeStruct((t, N_HEADS * HEAD_PAD), BF16),
            jax.ShapeDtypeStruct((N_HEADS * V_HEAD, t), BF16),
            jax.ShapeDtypeStruct((t, n_u), F32),
        ],
        compiler_params=_params(("arbitrary",)),
        name="in_proj",
    )(x2d, w["g_pre_mix"], w["w_in"], w["g_q_a"], w["w_q"], w["g_kv_a"], w["w_k"], w["w_vt"], cs)


def _flash_kernel(q_ref, k_ref, vt_ref, o_ref, s_ref, m_ref, l_ref, acc_ref, *, tk):
    n_k = k_ref.shape[1] // tk

    def scores(i, slot):
        start = pl.multiple_of(i * tk, tk)
        s_ref[slot] = lax.dot_general(k_ref[0, pl.ds(start, tk), :], q_ref[0],
                                      (((1,), (1,)), ((), ())), preferred_element_type=F32)

    def fold(x, op):
        x = x.reshape(tk // SUBLANES, SUBLANES, x.shape[-1])
        while x.shape[0] > 1:
            half = x.shape[0] // 2
            x = op(x[:half], x[half:])
        return x[0]

    def update(i, slot):
        s = s_ref[slot]
        m = m_ref[...]
        m_new = jnp.maximum(m, jnp.max(fold(s, jnp.maximum), axis=0, keepdims=True))
        alpha = jnp.exp2(m - m_new)
        p = jnp.exp2(s - m_new)
        l_ref[...] = alpha * l_ref[...] + fold(p, jnp.add)
        start = pl.multiple_of(i * tk, tk)
        pv = jnp.dot(vt_ref[:, pl.ds(start, tk)], p.astype(BF16), preferred_element_type=F32)
        acc_ref[...] = alpha * acc_ref[...] + pv
        m_ref[...] = m_new

    m_ref[...] = jnp.full_like(m_ref, -jnp.inf)
    l_ref[...] = jnp.zeros_like(l_ref)
    acc_ref[...] = jnp.zeros_like(acc_ref)
    scores(0, 0)

    def body(j, _):
        scores(2 * j + 1, 1)
        update(2 * j, 0)
        scores(2 * j + 2, 0)
        update(2 * j + 1, 1)
        return 0

    lax.fori_loop(0, n_k // 2 - 1, body, 0)
    scores(n_k - 1, 1)
    update(n_k - 2, 0)
    update(n_k - 1, 1)
    l = jnp.sum(l_ref[...], axis=0, keepdims=True)
    o_ref[0] = (acc_ref[...] / l).T.astype(o_ref.dtype)


def _attention(q, k, vt, batch, seq_len):
    q = q.reshape(batch, seq_len, N_HEADS * HEAD_PAD)
    k = k.reshape(batch, seq_len, N_HEADS * HEAD_PAD)
    tq = _tile(seq_len, 512)
    tk = _tile(seq_len // 2, 512)
    out = pl.pallas_call(
        functools.partial(_flash_kernel, tk=tk),
        grid=(batch, N_HEADS, seq_len // tq),
        in_specs=[
            pl.BlockSpec((1, tq, HEAD_PAD), lambda b, h, i: (b, i, h)),
            pl.BlockSpec((1, seq_len, HEAD_PAD), lambda b, h, i: (b, 0, h)),
            pl.BlockSpec((V_HEAD, seq_len), lambda b, h, i: (h, b)),
        ],
        out_specs=pl.BlockSpec((1, tq, V_HEAD), lambda b, h, i: (b, i, h)),
        out_shape=jax.ShapeDtypeStruct((batch, seq_len, N_HEADS * V_HEAD), BF16),
        scratch_shapes=[pltpu.VMEM((2, tk, tq), F32), pltpu.VMEM((1, tq), F32),
                        pltpu.VMEM((SUBLANES, tq), F32), pltpu.VMEM((V_HEAD, tq), F32)],
        compiler_params=_params(("arbitrary", "arbitrary", "arbitrary")),
        name="attention",
    )(q, k, vt)
    return out.reshape(batch * seq_len, N_HEADS * V_HEAD)


def _ssm_matrices(lam_re, lam_im, log_dt, b_re, b_im, c_re, c_im):
    t_c = SSM_CHUNK
    n_g = lam_re.shape[1]
    w_in, w_state, a_r, a_s = [], [], [], []
    toep = 0.0
    for d in range(2):
        dt = jnp.exp(log_dt[d].astype(F32))[:, None]
        lam = lax.complex(lam_re[d].astype(F32), lam_im[d].astype(F32))
        a = jnp.exp(lam * dt)
        coef = (a - 1.0) / lam
        bt = lax.complex(b_re[d].astype(F32), b_im[d].astype(F32)) * coef[:, :, None]
        cc = lax.complex(c_re[d].astype(F32), c_im[d].astype(F32))
        steps = jnp.arange(t_c + 1, dtype=F32)
        apow = jnp.exp(lam[None] * dt[None] * steps[:, None, None])
        in_pow = apow[:t_c][::-1] if d == 0 else apow[:t_c]
        win = jnp.einsum("sgp,gpc->gscp", in_pow, bt).reshape(n_g, LANES, SSM_STATE)
        w_in.append(jnp.concatenate([win.real, win.imag, win.imag, win.real], axis=-1))
        out_pow = apow[1:] if d == 0 else apow[1:][::-1]
        wst = jnp.einsum("gcp,tgp->gptc", cc, out_pow).reshape(n_g, SSM_STATE, LANES)
        w_state.append(jnp.concatenate([wst.real, -wst.imag], axis=1))
        kern = jnp.einsum("gcp,kgp,gpe->kgce", cc, apow[:t_c], bt).real
        s_idx = jnp.arange(t_c)[:, None]
        t_idx = jnp.arange(t_c)[None, :]
        lag = (t_idx - s_idx) if d == 0 else (s_idx - t_idx)
        blocks = jnp.where((lag >= 0)[:, :, None, None, None], kern[jnp.clip(lag, 0, t_c - 1)], 0.0)
        toep = toep + blocks.transpose(2, 0, 4, 1, 3).reshape(n_g, LANES, LANES)
        a_t = apow[t_c]
        a_r.append(jnp.concatenate([a_t.real, a_t.real], axis=-1))
        a_s.append(jnp.concatenate([-a_t.imag, a_t.imag], axis=-1))
    w_out = jnp.concatenate([toep, w_state[0], w_state[1]], axis=1)
    return dict(w_in_f=w_in[0].astype(BF16), w_in_b=w_in[1].astype(BF16), w_out=w_out.astype(BF16),
                ar_f=a_r[0], as_f=a_s[0], ar_b=a_r[1], as_b=a_s[1])


def _ssm_state_kernel(uf_ref, ub_ref, wf_ref, wb_ref, arf_ref, asf_ref, arb_ref, asb_ref,
                      xf_ref, xb_ref, vf, vft, vb, vbt, carry, *, tj, pitch):
    n_g = uf_ref.shape[1]

    @pl.when(pl.program_id(1) == 0)
    def _():
        carry[...] = jnp.zeros_like(carry)

    def project(g, _):
        row = pl.multiple_of(g * pitch, SUBLANES)
        pf = jnp.dot(uf_ref[0, g], wf_ref[g], preferred_element_type=F32)
        vf[pl.ds(row, tj), :] = pf[:, :LANES]
        vft[pl.ds(row, tj), :] = pf[:, LANES:]
        pb = jnp.dot(ub_ref[0, g], wb_ref[g], preferred_element_type=F32)
        vb[pl.ds(row, tj), :] = pb[:, :LANES]
        vbt[pl.ds(row, tj), :] = pb[:, LANES:]
        return 0

    lax.fori_loop(0, n_g, project, 0)

    n_blk = n_g // SUBLANES

    def scan(j, state):
        jb = tj - 1 - j
        new = []
        for blk in range(n_blk):
            base = blk * SUBLANES * pitch
            gs = pl.ds(blk * SUBLANES, SUBLANES)
            for d, (v, vt, ar_ref, as_ref, jj) in enumerate(
                    ((vf, vft, arf_ref, asf_ref, j), (vb, vbt, arb_ref, asb_ref, jb))):
                x, xt = state[(blk * 2 + d) * 2], state[(blk * 2 + d) * 2 + 1]
                rows = pl.ds(base + jj, SUBLANES, stride=pitch)
                vx = v[rows, :]
                vxt = vt[rows, :]
                v[rows, :] = x
                ar = ar_ref[gs, :]
                a_s = as_ref[gs, :]
                new.append(ar * x + a_s * xt + vx)
                new.append(ar * xt - a_s * x + vxt)
        return tuple(new)

    init = tuple(carry[i] for i in range(4 * n_blk))
    final = lax.fori_loop(0, tj, scan, init)
    for i in range(4 * n_blk):
        carry[i] = final[i]

    def emit(g, _):
        row = pl.multiple_of(g * pitch, SUBLANES)
        xf_ref[0, g] = vf[pl.ds(row, tj), :].astype(BF16)
        xb_ref[0, g] = vb[pl.ds(row, tj), :].astype(BF16)
        return 0

    lax.fori_loop(0, n_g, emit, 0)


def _ssm_out_kernel(u_ref, xf_ref, xb_ref, w_ref, y_ref):
    n_g = u_ref.shape[1]

    def body(g, _):
        lhs = jnp.concatenate([u_ref[0, g], xf_ref[0, g], xb_ref[0, g]], axis=1)
        y_ref[0, g] = jnp.dot(lhs, w_ref[g], preferred_element_type=F32)
        return 0

    lax.fori_loop(0, n_g, body, 0)


def _ssm(u2d, batch, seq_len, w):
    n_g = u2d.shape[1] // SSM_GROUP
    n_j = seq_len // SSM_CHUNK
    uc = (u2d.astype(BF16).reshape(batch, n_j, SSM_CHUNK, n_g, SSM_GROUP)
          .transpose(0, 3, 1, 2, 4).reshape(batch, n_g, n_j, LANES))
    tj = _tile(n_j, 128)
    n_t = n_j // tj
    pitch = tj + SSM_PITCH_PAD
    blk = (1, n_g, tj, LANES)
    fwd = lambda b, i: (b, 0, i, 0)
    bwd = lambda b, i: (b, 0, n_t - 1 - i, 0)
    scan_buf = pltpu.VMEM((n_g * pitch, LANES), F32)
    xf, xb = pl.pallas_call(
        functools.partial(_ssm_state_kernel, tj=tj, pitch=pitch),
        grid=(batch, n_t),
        in_specs=[
            pl.BlockSpec(blk, fwd),
            pl.BlockSpec(blk, bwd),
            _const_spec(w["w_in_f"].shape),
            _const_spec(w["w_in_b"].shape),
            _const_spec(w["ar_f"].shape),
            _const_spec(w["as_f"].shape),
            _const_spec(w["ar_b"].shape),
            _const_spec(w["as_b"].shape),
        ],
        out_specs=[pl.BlockSpec(blk, fwd), pl.BlockSpec(blk, bwd)],
        out_shape=[jax.ShapeDtypeStruct(uc.shape, BF16)] * 2,
        scratch_shapes=[scan_buf, scan_buf, scan_buf, scan_buf,
                        pltpu.VMEM((4 * (n_g // SUBLANES), SUBLANES, LANES), F32)],
        compiler_params=_params(("arbitrary", "arbitrary")),
        name="ssm_state",
    )(uc, uc, w["w_in_f"], w["w_in_b"], w["ar_f"], w["as_f"], w["ar_b"], w["as_b"])
    same = lambda b, i: (b, 0, i, 0)
    yc = pl.pallas_call(
        _ssm_out_kernel,
        grid=(batch, n_t),
        in_specs=[pl.BlockSpec(blk, same), pl.BlockSpec(blk, same), pl.BlockSpec(blk, same),
                  _const_spec(w["w_out"].shape)],
        out_specs=pl.BlockSpec(blk, same),
        out_shape=jax.ShapeDtypeStruct(uc.shape, F32),
        compiler_params=_params(("arbitrary", "arbitrary")),
        name="ssm_out",
    )(uc, xf, xb, w["w_out"])
    return (yc.reshape(batch, n_g, n_j, SSM_CHUNK, SSM_GROUP)
            .transpose(0, 2, 3, 1, 4).reshape(batch * seq_len, n_g * SSM_GROUP))


def _mix_out_kernel(ys_ref, u_ref, a_ref, x_ref, d_ref, wglu_ref, bglu_ref, ga_ref, gs_ref,
                    woa_ref, wos_ref, gpost_ref, gffn_ref, x1_ref, h2_ref):
    y = jax.nn.gelu(ys_ref[...] + d_ref[...] * u_ref[...])
    gate = jax.nn.sigmoid(jnp.dot(y.astype(BF16), wglu_ref[...], preferred_element_type=F32)
                          + bglu_ref[...])
    s = y * gate
    an = _rms(a_ref[...].astype(F32), ga_ref[...]).astype(BF16)
    sn = _rms(s, gs_ref[...]).astype(BF16)
    m = (jnp.dot(an, woa_ref[...], preferred_element_type=F32)
         + jnp.dot(sn, wos_ref[...], preferred_element_type=F32))
    x1 = x_ref[...] + _rms(m, gpost_ref[...])
    x1_ref[...] = x1
    h2_ref[...] = _rms(x1, gffn_ref[...]).astype(BF16)


def _mix_out(ys, u, a, x2d, w):
    t, d = x2d.shape
    n_a = a.shape[1]
    n_s = ys.shape[1]
    tm = _tile(t, 256)
    row = lambda i: (i, 0)
    return pl.pallas_call(
        _mix_out_kernel,
        grid=(t // tm,),
        in_specs=[
            pl.BlockSpec((tm, n_s), row),
            pl.BlockSpec((tm, n_s), row),
            pl.BlockSpec((tm, n_a), row),
            pl.BlockSpec((tm, d), row),
            _const_spec((1, n_s)),
            _const_spec(w["w_glu"].shape),
            _const_spec((1, n_s)),
            _const_spec((1, n_a)),
            _const_spec((1, n_s)),
            _const_spec(w["w_out_a"].shape),
            _const_spec(w["w_out_s"].shape),
            _const_spec((1, d)),
            _const_spec((1, d)),
        ],
        out_specs=[pl.BlockSpec((tm, d), row), pl.BlockSpec((tm, d), row)],
        out_shape=[jax.ShapeDtypeStruct((t, d), F32), jax.ShapeDtypeStruct((t, d), BF16)],
        compiler_params=_params(("arbitrary",)),
        name="mix_out",
    )(ys, u, a, x2d, w["ssm_d"], w["w_glu"], w["b_glu"], w["g_out_attn"], w["g_out_ssm"],
      w["w_out_a"], w["w_out_s"], w["g_post_mix"], w["g_pre_ffn"])


def _ffn_kernel(h_ref, x1_ref, wg_ref, wu_ref, wd_ref, gpost_ref, o_ref, acc_ref):
    j = pl.program_id(1)

    @pl.when(j == 0)
    def _():
        acc_ref[...] = jnp.zeros_like(acc_ref)

    h = h_ref[...]
    gate = jnp.dot(h, wg_ref[...], preferred_element_type=F32)
    up = jnp.dot(h, wu_ref[...], preferred_element_type=F32)
    act = (jax.nn.silu(gate) * up).astype(BF16)
    acc_ref[...] += jnp.dot(act, wd_ref[...], preferred_element_type=F32)

    @pl.when(j == pl.num_programs(1) - 1)
    def _():
        o_ref[...] = x1_ref[...] + _rms(acc_ref[...], gpost_ref[...])


def _ffn(h2, x1, w):
    t, d = x1.shape
    d_ff = w["w_gate"].shape[1]
    tm = _tile(t, 512)
    tf = _tile(d_ff, 512)
    return pl.pallas_call(
        _ffn_kernel,
        grid=(t // tm, d_ff // tf),
        in_specs=[
            pl.BlockSpec((tm, d), lambda i, j: (i, 0)),
            pl.BlockSpec((tm, d), lambda i, j: (i, 0)),
            pl.BlockSpec((d, tf), lambda i, j: (0, j)),
            pl.BlockSpec((d, tf), lambda i, j: (0, j)),
            pl.BlockSpec((tf, d), lambda i, j: (j, 0)),
            pl.BlockSpec((1, d), lambda i, j: (0, 0)),
        ],
        out_specs=pl.BlockSpec((tm, d), lambda i, j: (i, 0)),
        out_shape=jax.ShapeDtypeStruct((t, d), F32),
        scratch_shapes=[pltpu.VMEM((tm, d), F32)],
        compiler_params=_params(("arbitrary", "arbitrary")),
        name="ffn",
    )(h2, x1, w["w_gate"], w["w_up"], w["w_down"], w["g_post_ffn"])


def _rope_table(length):
    inv = 1.0 / (ROPE_THETA ** (jnp.arange(0, QK_ROPE, 2, dtype=F32) / QK_ROPE))
    ang = jnp.arange(length, dtype=F32)[:, None] * inv[None, :]
    cos, sin = jnp.cos(ang), jnp.sin(ang)
    return jnp.concatenate([cos, cos, -sin, sin], axis=-1)


def _swap_halves(wr):
    half = QK_ROPE // 2
    return jnp.concatenate([wr[..., half:], wr[..., :half]], axis=-1)


def _prepare_weights(g_pre_mix, w_in, g_q_a, w_q_b, g_kv_a, w_kv_b, lam_re, lam_im, log_dt,
                     b_re, b_im, c_re, c_im, ssm_d, w_glu, b_glu, g_out_attn, g_out_ssm, w_out,
                     g_post_mix, g_pre_ffn, w_gate, w_up, w_down, g_post_ffn):
    row = lambda g: g.astype(F32).reshape(1, -1)
    i1 = Q_LORA + KV_LORA
    i2 = i1 + QK_ROPE
    w_rope = w_in[:, i1:i2]
    w_in_ext = jnp.concatenate([w_in[:, :i1], w_in[:, i2:], w_rope, _swap_halves(w_rope)], axis=1)
    wq = w_q_b.reshape(Q_LORA, N_HEADS, QK_NOPE + QK_ROPE)
    wq_ext = jnp.concatenate([wq, _swap_halves(wq[..., QK_NOPE:])], axis=-1)
    wkv = w_kv_b.reshape(KV_LORA, N_HEADS, QK_NOPE + V_HEAD)
    n_a = N_HEADS * V_HEAD
    out = dict(
        n_u=w_in.shape[1] - i2,
        q_scale=float((QK_NOPE + QK_ROPE) ** -0.5 * math.log2(math.e)),
        g_pre_mix=row(g_pre_mix), w_in=w_in_ext.astype(BF16),
        g_q_a=row(g_q_a), w_q=wq_ext.reshape(Q_LORA, N_HEADS * HEAD_PAD).astype(BF16),
        g_kv_a=row(g_kv_a),
        w_k=wkv[..., :QK_NOPE].reshape(KV_LORA, N_HEADS * QK_NOPE).astype(BF16),
        w_vt=wkv[..., QK_NOPE:].reshape(KV_LORA, N_HEADS * V_HEAD).T.astype(BF16),
        ssm_d=row(ssm_d), w_glu=w_glu.astype(BF16), b_glu=row(b_glu),
        g_out_attn=row(g_out_attn), g_out_ssm=row(g_out_ssm),
        w_out_a=w_out[:n_a].astype(BF16), w_out_s=w_out[n_a:].astype(BF16),
        g_post_mix=row(g_post_mix), g_pre_ffn=row(g_pre_ffn),
        w_gate=w_gate.astype(BF16), w_up=w_up.astype(BF16), w_down=w_down.astype(BF16),
        g_post_ffn=row(g_post_ffn),
    )
    out.update(_ssm_matrices(lam_re, lam_im, log_dt, b_re, b_im, c_re, c_im))
    return out


def _layer(x, w):
    batch, seq_len, d = x.shape
    x2d = x.reshape(batch * seq_len, d)
    q, k, vt, u = _in_proj(x2d, seq_len, w, _rope_table(seq_len))
    a = _attention(q, k, vt, batch, seq_len)
    ys = _ssm(u, batch, seq_len, w)
    x1, h2 = _mix_out(ys, u, a, x2d, w)
    return _ffn(h2, x1, w).reshape(batch, seq_len, d)


def kernel(x_prompt, x_sample, g_pre_mix, w_in, g_q_a, w_q_b, g_kv_a, w_kv_b, ssm_lam_re, ssm_lam_im, ssm_log_dt, ssm_b_re, ssm_b_im, ssm_c_re, ssm_c_im, ssm_d, w_glu, b_glu, g_out_attn, g_out_ssm, w_out, g_post_mix, g_pre_ffn, w_gate, w_up, w_down, g_post_ffn):
    weights = (g_pre_mix, w_in, g_q_a, w_q_b, g_kv_a, w_kv_b, ssm_lam_re, ssm_lam_im, ssm_log_dt,
               ssm_b_re, ssm_b_im, ssm_c_re, ssm_c_im, ssm_d, w_glu, b_glu, g_out_attn, g_out_ssm,
               w_out, g_post_mix, g_pre_ffn, w_gate, w_up, w_down, g_post_ffn)
    depth = g_pre_mix.shape[0]
    for layer in range(depth):
        w = _prepare_weights(*[p[layer] for p in weights])
        x_prompt = _layer(x_prompt, w)
        x_sample = _layer(x_sample, w)
    return (x_prompt, x_sample)
```

```python
import functools
import math

import jax
import jax.numpy as jnp
from jax import lax
from jax.experimental import pallas as pl
from jax.experimental.pallas import tpu as pltpu

F32 = jnp.float32
BF16 = jnp.bfloat16

RMS_EPS = 1e-6
ROPE_THETA = 10000.0
LANES = 128
SUBLANES = 8
VMEM_LIMIT_BYTES = 56 * 1024 * 1024

N_HEADS = 8
QK_NOPE = 128
QK_ROPE = 64
V_HEAD = 128
HEAD_PAD = 256
Q_LORA = 512
KV_LORA = 256
ACC_ROWS = V_HEAD + 16
SSM_GROUP = 16
SSM_STATE = 64
SSM_CHUNK = LANES // SSM_GROUP
SSM_PITCH_PAD = 8


def _rms(x, g):
    return x * lax.rsqrt(jnp.mean(x * x, axis=-1, keepdims=True) + RMS_EPS) * g


def _params(sem):
    return pltpu.CompilerParams(dimension_semantics=sem, vmem_limit_bytes=VMEM_LIMIT_BYTES)


def _tile(n, target):
    t = min(n, target)
    while n % t:
        t //= 2
    return t


def _const_spec(shape):
    nd = len(shape)
    return pl.BlockSpec(shape, lambda *_: (0,) * nd, pipeline_mode=pl.Buffered(1))


def _in_proj_kernel(x_ref, gpre_ref, win_ref, gq_ref, wq_ref, gkv_ref, wk_ref, wvt_ref, cs_ref,
                    q_ref, k_ref, vt_ref, u_ref, *, q_scale):
    x = x_ref[...]
    h = _rms(x, gpre_ref[...])
    z = jnp.dot(h.astype(BF16), win_ref[...], preferred_element_type=F32)
    c_q = z[:, :Q_LORA]
    c_kv = z[:, Q_LORA:Q_LORA + KV_LORA]
    n_u = u_ref.shape[1]
    u_ref[...] = z[:, Q_LORA + KV_LORA:Q_LORA + KV_LORA + n_u]
    k_rope = z[:, Q_LORA + KV_LORA + n_u:]
    cs = cs_ref[...]

    def rotate(t):
        t = t * cs
        return t + pltpu.roll(t, QK_ROPE, 1)

    cqn = _rms(c_q, gq_ref[...]).astype(BF16)
    q = jnp.dot(cqn, wq_ref[...], preferred_element_type=F32) * q_scale
    for hd in range(N_HEADS):
        lo = hd * HEAD_PAD
        q_ref[:, lo:lo + QK_NOPE] = q[:, lo:lo + QK_NOPE].astype(BF16)
        q_ref[:, lo + QK_NOPE:lo + HEAD_PAD] = rotate(q[:, lo + QK_NOPE:lo + HEAD_PAD]).astype(BF16)

    ckvn = _rms(c_kv, gkv_ref[...]).astype(BF16)
    k_nope = jnp.dot(ckvn, wk_ref[...], preferred_element_type=F32)
    vt_ref[...] = lax.dot_general(wvt_ref[...], ckvn, (((1,), (1,)), ((), ())),
                                  preferred_element_type=F32).astype(BF16)
    lane = lax.broadcasted_iota(jnp.int32, k_rope.shape, 1)
    kr = jnp.where(lane < QK_ROPE, rotate(k_rope), 0.0).astype(BF16)
    for hd in range(N_HEADS):
        lo = hd * HEAD_PAD
        k_ref[:, lo:lo + QK_NOPE] = k_nope[:, hd * QK_NOPE:(hd + 1) * QK_NOPE].astype(BF16)
        k_ref[:, lo + QK_NOPE:lo + HEAD_PAD] = kr


def _in_proj(x2d, seq_len, w, cs):
    t, d = x2d.shape
    tm = _tile(seq_len, 512)
    n_u = w["n_u"]
    kern = functools.partial(_in_proj_kernel, q_scale=w["q_scale"])
    tiles_per_seq = seq_len // tm
    row = lambda i: (i, 0)
    return pl.pallas_call(
        kern,
        grid=(t // tm,),
        in_specs=[
            pl.BlockSpec((tm, d), row),
            _const_spec((1, d)),
            _const_spec(w["w_in"].shape),
            _const_spec((1, Q_LORA)),
            _const_spec(w["w_q"].shape),
            _const_spec((1, KV_LORA)),
            _const_spec(w["w_k"].shape),
            _const_spec(w["w_vt"].shape),
            pl.BlockSpec((tm, LANES), lambda i: (i % tiles_per_seq, 0)),
        ],
        out_specs=[
            pl.BlockSpec((tm, N_HEADS * HEAD_PAD), row),
            pl.BlockSpec((tm, N_HEADS * HEAD_PAD), row),
            pl.BlockSpec((N_HEADS * V_HEAD, tm), lambda i: (0, i)),
            pl.BlockSpec((tm, n_u), row),
        ],
        out_shape=[
            jax.ShapeDtypeStruct((t, N_HEADS * HEAD_PAD), BF16),
            jax.ShapeDtypeStruct((t, N_HEADS * HEAD_PAD), BF16),
            jax.ShapeDtypeStruct((N_HEADS * V_HEAD, t), BF16),
            jax.ShapeDtypeStruct((t, n_u), F32),
        ],
        compiler_params=_params(("arbitrary",)),
        name="in_proj",
    )(x2d, w["g_pre_mix"], w["w_in"], w["g_q_a"], w["w_q"], w["g_kv_a"], w["w_k"], w["w_vt"], cs)


def _flash_kernel(q_ref, k_ref, vt_ref, o_ref, s_ref, mx_ref, m_ref, acc_ref, *, tk):
    n_k = k_ref.shape[1] // tk
    ones = jnp.ones((ACC_ROWS - V_HEAD, tk), BF16)

    def scores(i, slot):
        start = pl.multiple_of(i * tk, tk)
        s = lax.dot_general(k_ref[0, pl.ds(start, tk), :], q_ref[0],
                            (((1,), (1,)), ((), ())), preferred_element_type=F32)
        s_ref[slot] = s
        x = s.reshape(tk // SUBLANES, SUBLANES, s.shape[-1])
        while x.shape[0] > 1:
            half = x.shape[0] // 2
            x = jnp.maximum(x[:half], x[half:])
        mx_ref[slot] = x[0]

    def update(i, slot):
        m = m_ref[...]
        m_new = jnp.maximum(m, jnp.max(mx_ref[slot], axis=0, keepdims=True))
        alpha = jnp.exp2(m - m_new)
        p = jnp.exp2(s_ref[slot] - m_new).astype(BF16)
        start = pl.multiple_of(i * tk, tk)
        v_aug = jnp.concatenate([vt_ref[:, pl.ds(start, tk)], ones], axis=0)
        acc_ref[...] = alpha * acc_ref[...] + jnp.dot(v_aug, p, preferred_element_type=F32)
        m_ref[...] = m_new

    m_ref[...] = jnp.full_like(m_ref, -jnp.inf)
    acc_ref[...] = jnp.zeros_like(acc_ref)
    scores(0, 0)

    def body(j, _):
        scores(2 * j + 1, 1)
        update(2 * j, 0)
        scores(2 * j + 2, 0)
        update(2 * j + 1, 1)
        return 0

    lax.fori_loop(0, n_k // 2 - 1, body, 0)
    scores(n_k - 1, 1)
    update(n_k - 2, 0)
    update(n_k - 1, 1)
    acc = acc_ref[...]
    o_ref[0] = (acc[:V_HEAD] / acc[V_HEAD:V_HEAD + 1]).T.astype(o_ref.dtype)


def _attention(q, k, vt, batch, seq_len):
    q = q.reshape(batch, seq_len, N_HEADS * HEAD_PAD)
    k = k.reshape(batch, seq_len, N_HEADS * HEAD_PAD)
    tq = _tile(seq_len, 512)
    tk = _tile(seq_len // 2, 1024)
    out = pl.pallas_call(
        functools.partial(_flash_kernel, tk=tk),
        grid=(batch, N_HEADS, seq_len // tq),
        in_specs=[
            pl.BlockSpec((1, tq, HEAD_PAD), lambda b, h, i: (b, i, h)),
            pl.BlockSpec((1, seq_len, HEAD_PAD), lambda b, h, i: (b, 0, h)),
            pl.BlockSpec((V_HEAD, seq_len), lambda b, h, i: (h, b)),
        ],
        out_specs=pl.BlockSpec((1, tq, V_HEAD), lambda b, h, i: (b, i, h)),
        out_shape=jax.ShapeDtypeStruct((batch, seq_len, N_HEADS * V_HEAD), BF16),
        scratch_shapes=[pltpu.VMEM((2, tk, tq), F32), pltpu.VMEM((2, SUBLANES, tq), F32),
                        pltpu.VMEM((1, tq), F32), pltpu.VMEM((ACC_ROWS, tq), F32)],
        compiler_params=_params(("arbitrary", "arbitrary", "arbitrary")),
        name="attention",
    )(q, k, vt)
    return out.reshape(batch * seq_len, N_HEADS * V_HEAD)


def _ssm_matrices(lam_re, lam_im, log_dt, b_re, b_im, c_re, c_im):
    t_c = SSM_CHUNK
    n_g = lam_re.shape[1]
    w_in, w_state, a_r, a_s = [], [], [], []
    toep = 0.0
    for d in range(2):
        dt = jnp.exp(log_dt[d].astype(F32))[:, None]
        lam = lax.complex(lam_re[d].astype(F32), lam_im[d].astype(F32))
        a = jnp.exp(lam * dt)
        coef = (a - 1.0) / lam
        bt = lax.complex(b_re[d].astype(F32), b_im[d].astype(F32)) * coef[:, :, None]
        cc = lax.complex(c_re[d].astype(F32), c_im[d].astype(F32))
        steps = jnp.arange(t_c + 1, dtype=F32)
        apow = jnp.exp(lam[None] * dt[None] * steps[:, None, None])
        in_pow = apow[:t_c][::-1] if d == 0 else apow[:t_c]
        win = jnp.einsum("sgp,gpc->gscp", in_pow, bt).reshape(n_g, LANES, SSM_STATE)
        w_in.append(jnp.concatenate([win.real, win.imag, win.imag, win.real], axis=-1))
        out_pow = apow[1:] if d == 0 else apow[1:][::-1]
        wst = jnp.einsum("gcp,tgp->gptc", cc, out_pow).reshape(n_g, SSM_STATE, LANES)
        w_state.append(jnp.concatenate([wst.real, -wst.imag], axis=1))
        kern = jnp.einsum("gcp,kgp,gpe->kgce", cc, apow[:t_c], bt).real
        s_idx = jnp.arange(t_c)[:, None]
        t_idx = jnp.arange(t_c)[None, :]
        lag = (t_idx - s_idx) if d == 0 else (s_idx - t_idx)
        blocks = jnp.where((lag >= 0)[:, :, None, None, None], kern[jnp.clip(lag, 0, t_c - 1)], 0.0)
        toep = toep + blocks.transpose(2, 0, 4, 1, 3).reshape(n_g, LANES, LANES)
        a_t = apow[t_c]
        a_r.append(jnp.concatenate([a_t.real, a_t.real], axis=-1))
        a_s.append(jnp.concatenate([-a_t.imag, a_t.imag], axis=-1))
    w_out = jnp.concatenate([toep, w_state[0], w_state[1]], axis=1)
    return dict(w_in_f=w_in[0].astype(BF16), w_in_b=w_in[1].astype(BF16), w_out=w_out.astype(BF16),
                ar_f=a_r[0], as_f=a_s[0], ar_b=a_r[1], as_b=a_s[1])


def _ssm_state_kernel(uf_ref, ub_ref, wf_ref, wb_ref, arf_ref, asf_ref, arb_ref, asb_ref,
                      xf_ref, xb_ref, vf, vft, vb, vbt, carry, *, tj, pitch):
    n_g = uf_ref.shape[1]

    @pl.when(pl.program_id(1) == 0)
    def _():
        carry[...] = jnp.zeros_like(carry)

    def project(g, _):
        row = pl.multiple_of(g * pitch, SUBLANES)
        pf = jnp.dot(uf_ref[0, g], wf_ref[g], preferred_element_type=F32)
        vf[pl.ds(row, tj), :] = pf[:, :LANES]
        vft[pl.ds(row, tj), :] = pf[:, LANES:]
        pb = jnp.dot(ub_ref[0, g], wb_ref[g], preferred_element_type=F32)
        vb[pl.ds(row, tj), :] = pb[:, :LANES]
        vbt[pl.ds(row, tj), :] = pb[:, LANES:]
        return 0

    lax.fori_loop(0, n_g, project, 0)

    n_blk = n_g // SUBLANES

    def scan(j, state):
        jb = tj - 1 - j
        new = []
        for blk in range(n_blk):
            base = blk * SUBLANES * pitch
            gs = pl.ds(blk * SUBLANES, SUBLANES)
            for d, (v, vt, ar_ref, as_ref, jj) in enumerate(
                    ((vf, vft, arf_ref, asf_ref, j), (vb, vbt, arb_ref, asb_ref, jb))):
                x, xt = state[(blk * 2 + d) * 2], state[(blk * 2 + d) * 2 + 1]
                rows = pl.ds(base + jj, SUBLANES, stride=pitch)
                vx = v[rows, :]
                vxt = vt[rows, :]
                v[rows, :] = x
                ar = ar_ref[gs, :]
                a_s = as_ref[gs, :]
                new.append(ar * x + a_s * xt + vx)
                new.append(ar * xt - a_s * x + vxt)
        return tuple(new)

    init = tuple(carry[i] for i in range(4 * n_blk))
    final = lax.fori_loop(0, tj, scan, init)
    for i in range(4 * n_blk):
        carry[i] = final[i]

    def emit(g, _):
        row = pl.multiple_of(g * pitch, SUBLANES)
        xf_ref[0, g] = vf[pl.ds(row, tj), :].astype(BF16)
        xb_ref[0, g] = vb[pl.ds(row, tj), :].astype(BF16)
        return 0

    lax.fori_loop(0, n_g, emit, 0)


def _ssm_out_kernel(u_ref, xf_ref, xb_ref, w_ref, y_ref):
    n_g = u_ref.shape[1]

    def body(g, _):
        lhs = jnp.concatenate([u_ref[0, g], xf_ref[0, g], xb_ref[0, g]], axis=1)
        y_ref[0, g] = jnp.dot(lhs, w_ref[g], preferred_element_type=F32)
        return 0

    lax.fori_loop(0, n_g, body, 0)


def _ssm(u2d, batch, seq_len, w):
    n_g = u2d.shape[1] // SSM_GROUP
    n_j = seq_len // SSM_CHUNK
    uc = (u2d.astype(BF16).reshape(batch, n_j, SSM_CHUNK, n_g, SSM_GROUP)
          .transpose(0, 3, 1, 2, 4).reshape(batch, n_g, n_j, LANES))
    tj = _tile(n_j, 128)
    n_t = n_j // tj
    pitch = tj + SSM_PITCH_PAD
    blk = (1, n_g, tj, LANES)
    fwd = lambda b, i: (b, 0, i, 0)
    bwd = lambda b, i: (b, 0, n_t - 1 - i, 0)
    scan_buf = pltpu.VMEM((n_g * pitch, LANES), F32)
    xf, xb = pl.pallas_call(
        functools.partial(_ssm_state_kernel, tj=tj, pitch=pitch),
        grid=(batch, n_t),
        in_specs=[
            pl.BlockSpec(blk, fwd),
            pl.BlockSpec(blk, bwd),
            _const_spec(w["w_in_f"].shape),
            _const_spec(w["w_in_b"].shape),
            _const_spec(w["ar_f"].shape),
            _const_spec(w["as_f"].shape),
            _const_spec(w["ar_b"].shape),
            _const_spec(w["as_b"].shape),
        ],
        out_specs=[pl.BlockSpec(blk, fwd), pl.BlockSpec(blk, bwd)],
        out_shape=[jax.ShapeDtypeStruct(uc.shape, BF16)] * 2,
        scratch_shapes=[scan_buf, scan_buf, scan_buf, scan_buf,
                        pltpu.VMEM((4 * (n_g // SUBLANES), SUBLANES, LANES), F32)],
        compiler_params=_params(("arbitrary", "arbitrary")),
        name="ssm_state",
    )(uc, uc, w["w_in_f"], w["w_in_b"], w["ar_f"], w["as_f"], w["ar_b"], w["as_b"])
    same = lambda b, i: (b, 0, i, 0)
    yc = pl.pallas_call(
        _ssm_out_kernel,
        grid=(batch, n_t),
        in_specs=[pl.BlockSpec(blk, same), pl.BlockSpec(blk, same), pl.BlockSpec(blk, same),
                  _const_spec(w["w_out"].shape)],
        out_specs=pl.BlockSpec(blk, same),
        out_shape=jax.ShapeDtypeStruct(uc.shape, F32),
        compiler_params=_params(("arbitrary", "arbitrary")),
        name="ssm_out",
    )(uc, xf, xb, w["w_out"])
    return (yc.reshape(batch, n_g, n_j, SSM_CHUNK, SSM_GROUP)
            .transpose(0, 2, 3, 1, 4).reshape(batch * seq_len, n_g * SSM_GROUP))


def _mix_out_kernel(ys_ref, u_ref, a_ref, x_ref, d_ref, wglu_ref, bglu_ref, ga_ref, gs_ref,
                    woa_ref, wos_ref, gpost_ref, gffn_ref, x1_ref, h2_ref):
    y = jax.nn.gelu(ys_ref[...] + d_ref[...] * u_ref[...])
    gate = jax.nn.sigmoid(jnp.dot(y.astype(BF16), wglu_ref[...], preferred_element_type=F32)
                          + bglu_ref[...])
    s = y * gate
    an = _rms(a_ref[...].astype(F32), ga_ref[...]).astype(BF16)
    sn = _rms(s, gs_ref[...]).astype(BF16)
    m = (jnp.dot(an, woa_ref[...], preferred_element_type=F32)
         + jnp.dot(sn, wos_ref[...], preferred_element_type=F32))
    x1 = x_ref[...] + _rms(m, gpost_ref[...])
    x1_ref[...] = x1
    h2_ref[...] = _rms(x1, gffn_ref[...]).astype(BF16)


def _mix_out(ys, u, a, x2d, w):
    t, d = x2d.shape
    n_a = a.shape[1]
    n_s = ys.shape[1]
    tm = _tile(t, 256)
    row = lambda i: (i, 0)
    return pl.pallas_call(
        _mix_out_kernel,
        grid=(t // tm,),
        in_specs=[
            pl.BlockSpec((tm, n_s), row),
            pl.BlockSpec((tm, n_s), row),
            pl.BlockSpec((tm, n_a), row),
            pl.BlockSpec((tm, d), row),
            _const_spec((1, n_s)),
            _const_spec(w["w_glu"].shape),
            _const_spec((1, n_s)),
            _const_spec((1, n_a)),
            _const_spec((1, n_s)),
            _const_spec(w["w_out_a"].shape),
            _const_spec(w["w_out_s"].shape),
            _const_spec((1, d)),
            _const_spec((1, d)),
        ],
        out_specs=[pl.BlockSpec((tm, d), row), pl.BlockSpec((tm, d), row)],
        out_shape=[jax.ShapeDtypeStruct((t, d), F32), jax.ShapeDtypeStruct((t, d), BF16)],
        compiler_params=_params(("arbitrary",)),
        name="mix_out",
    )(ys, u, a, x2d, w["ssm_d"], w["w_glu"], w["b_glu"], w["g_out_attn"], w["g_out_ssm"],
      w["w_out_a"], w["w_out_s"], w["g_post_mix"], w["g_pre_ffn"])


def _ffn_kernel(h_ref, x1_ref, wg_ref, wu_ref, wd_ref, gpost_ref, o_ref, acc_ref):
    j = pl.program_id(1)

    @pl.when(j == 0)
    def _():
        acc_ref[...] = jnp.zeros_like(acc_ref)

    h = h_ref[...]
    gate = jnp.dot(h, wg_ref[...], preferred_element_type=F32)
    up = jnp.dot(h, wu_ref[...], preferred_element_type=F32)
    act = (jax.nn.silu(gate) * up).astype(BF16)
    acc_ref[...] += jnp.dot(act, wd_ref[...], preferred_element_type=F32)

    @pl.when(j == pl.num_programs(1) - 1)
    def _():
        o_ref[...] = x1_ref[...] + _rms(acc_ref[...], gpost_ref[...])


def _ffn(h2, x1, w):
    t, d = x1.shape
    d_ff = w["w_gate"].shape[1]
    tm = _tile(t, 512)
    tf = _tile(d_ff, 512)
    return pl.pallas_call(
        _ffn_kernel,
        grid=(t // tm, d_ff // tf),
        in_specs=[
            pl.BlockSpec((tm, d), lambda i, j: (i, 0)),
            pl.BlockSpec((tm, d), lambda i, j: (i, 0)),
            pl.BlockSpec((d, tf), lambda i, j: (0, j)),
            pl.BlockSpec((d, tf), lambda i, j: (0, j)),
            pl.BlockSpec((tf, d), lambda i, j: (j, 0)),
            pl.BlockSpec((1, d), lambda i, j: (0, 0)),
        ],
        out_specs=pl.BlockSpec((tm, d), lambda i, j: (i, 0)),
        out_shape=jax.ShapeDtypeStruct((t, d), F32),
        scratch_shapes=[pltpu.VMEM((tm, d), F32)],
        compiler_params=_params(("arbitrary", "arbitrary")),
        name="ffn",
    )(h2, x1, w["w_gate"], w["w_up"], w["w_down"], w["g_post_ffn"])


def _rope_table(length):
    inv = 1.0 / (ROPE_THETA ** (jnp.arange(0, QK_ROPE, 2, dtype=F32) / QK_ROPE))
    ang = jnp.arange(length, dtype=F32)[:, None] * inv[None, :]
    cos, sin = jnp.cos(ang), jnp.sin(ang)
    return jnp.concatenate([cos, cos, -sin, sin], axis=-1)


def _swap_halves(wr):
    half = QK_ROPE // 2
    return jnp.concatenate([wr[..., half:], wr[..., :half]], axis=-1)


def _prepare_weights(g_pre_mix, w_in, g_q_a, w_q_b, g_kv_a, w_kv_b, lam_re, lam_im, log_dt,
                     b_re, b_im, c_re, c_im, ssm_d, w_glu, b_glu, g_out_attn, g_out_ssm, w_out,
                     g_post_mix, g_pre_ffn, w_gate, w_up, w_down, g_post_ffn):
    row = lambda g: g.astype(F32).reshape(1, -1)
    i1 = Q_LORA + KV_LORA
    i2 = i1 + QK_ROPE
    w_rope = w_in[:, i1:i2]
    w_in_ext = jnp.concatenate([w_in[:, :i1], w_in[:, i2:], w_rope, _swap_halves(w_rope)], axis=1)
    wq = w_q_b.reshape(Q_LORA, N_HEADS, QK_NOPE + QK_ROPE)
    wq_ext = jnp.concatenate([wq, _swap_halves(wq[..., QK_NOPE:])], axis=-1)
    wkv = w_kv_b.reshape(KV_LORA, N_HEADS, QK_NOPE + V_HEAD)
    n_a = N_HEADS * V_HEAD
    out = dict(
        n_u=w_in.shape[1] - i2,
        q_scale=float((QK_NOPE + QK_ROPE) ** -0.5 * math.log2(math.e)),
        g_pre_mix=row(g_pre_mix), w_in=w_in_ext.astype(BF16),
        g_q_a=row(g_q_a), w_q=wq_ext.reshape(Q_LORA, N_HEADS * HEAD_PAD).astype(BF16),
        g_kv_a=row(g_kv_a),
        w_k=wkv[..., :QK_NOPE].reshape(KV_LORA, N_HEADS * QK_NOPE).astype(BF16),
        w_vt=wkv[..., QK_NOPE:].reshape(KV_LORA, N_HEADS * V_HEAD).T.astype(BF16),
        ssm_d=row(ssm_d), w_glu=w_glu.astype(BF16), b_glu=row(b_glu),
        g_out_attn=row(g_out_attn), g_out_ssm=row(g_out_ssm),
        w_out_a=w_out[:n_a].astype(BF16), w_out_s=w_out[n_a:].astype(BF16),
        g_post_mix=row(g_post_mix), g_pre_ffn=row(g_pre_ffn),
        w_gate=w_gate.astype(BF16), w_up=w_up.astype(BF16), w_down=w_down.astype(BF16),
        g_post_ffn=row(g_post_ffn),
    )
    out.update(_ssm_matrices(lam_re, lam_im, log_dt, b_re, b_im, c_re, c_im))
    return out


def _layer(x, w):
    batch, seq_len, d = x.shape
    x2d = x.reshape(batch * seq_len, d)
    q, k, vt, u = _in_proj(x2d, seq_len, w, _rope_table(seq_len))
    a = _attention(q, k, vt, batch, seq_len)
    ys = _ssm(u, batch, seq_len, w)
    x1, h2 = _mix_out(ys, u, a, x2d, w)
    return _ffn(h2, x1, w).reshape(batch, seq_len, d)


def kernel(x_prompt, x_sample, g_pre_mix, w_in, g_q_a, w_q_b, g_kv_a, w_kv_b, ssm_lam_re, ssm_lam_im, ssm_log_dt, ssm_b_re, ssm_b_im, ssm_c_re, ssm_c_im, ssm_d, w_glu, b_glu, g_out_attn, g_out_ssm, w_out, g_post_mix, g_pre_ffn, w_gate, w_up, w_down, g_post_ffn):
    weights = (g_pre_mix, w_in, g_q_a, w_q_b, g_kv_a, w_kv_b, ssm_lam_re, ssm_lam_im, ssm_log_dt,
               ssm_b_re, ssm_b_im, ssm_c_re, ssm_c_im, ssm_d, w_glu, b_glu, g_out_attn, g_out_ssm,
               w_out, g_post_mix, g_pre_ffn, w_gate, w_up, w_down, g_post_ffn)
    depth = g_pre_mix.shape[0]
    for layer in range(depth):
        w = _prepare_weights(*[p[layer] for p in weights])
        x_prompt = _layer(x_prompt, w)
        x_sample = _layer(x_sample, w)
    return (x_prompt, x_sample)
```

```python
import functools
import math

import jax
import jax.numpy as jnp
from jax import lax
from jax.experimental import pallas as pl
from jax.experimental.pallas import tpu as pltpu

F32 = jnp.float32
BF16 = jnp.bfloat16

RMS_EPS = 1e-6
ROPE_THETA = 10000.0
LANES = 128
SUBLANES = 8
VMEM_LIMIT_BYTES = 56 * 1024 * 1024

N_HEADS = 8
QK_NOPE = 128
QK_ROPE = 64
V_HEAD = 128
HEAD_PAD = 256
Q_LORA = 512
KV_LORA = 256
ACC_ROWS = V_HEAD + 16
SSM_GROUP = 16
SSM_STATE = 64
SSM_CHUNK = LANES // SSM_GROUP
SSM_PITCH_PAD = 8


def _rms(x, g):
    return x * lax.rsqrt(jnp.mean(x * x, axis=-1, keepdims=True) + RMS_EPS) * g


def _params(sem):
    return pltpu.CompilerParams(dimension_semantics=sem, vmem_limit_bytes=VMEM_LIMIT_BYTES)


def _tile(n, target):
    t = min(n, target)
    while n % t:
        t //= 2
    return t


def _block_transpose(a):
    lane = lax.broadcasted_iota(jnp.int32, a[0].shape, 1)
    for d in (4, 2, 1):
        shift = SSM_GROUP * d
        upper = (lane & shift) != 0
        new = list(a)
        for i in range(len(a)):
            if i & d:
                continue
            x, y = a[i], a[i | d]
            new[i] = jnp.where(upper, pltpu.roll(y, shift, 1), x)
            new[i | d] = jnp.where(upper, y, pltpu.roll(x, LANES - shift, 1))
        a = new
    return a


def _const_spec(shape):
    nd = len(shape)
    return pl.BlockSpec(shape, lambda *_: (0,) * nd, pipeline_mode=pl.Buffered(1))


def _in_proj_kernel(x_ref, gpre_ref, win_ref, gq_ref, wq_ref, gkv_ref, wk_ref, wvt_ref, cs_ref,
                    q_ref, k_ref, vt_ref, u_ref, uc_ref, *, q_scale):
    x = x_ref[...]
    h = _rms(x, gpre_ref[...])
    z = jnp.dot(h.astype(BF16), win_ref[...], preferred_element_type=F32)
    c_q = z[:, :Q_LORA]
    c_kv = z[:, Q_LORA:Q_LORA + KV_LORA]
    n_blk = u_ref.shape[0]
    u0 = Q_LORA + KV_LORA
    for blk in range(n_blk):
        u_ref[blk] = z[:, u0 + blk * LANES:u0 + (blk + 1) * LANES]
    k_rope = z[:, u0 + n_blk * LANES:]
    cs = cs_ref[...]

    def rotate(t):
        t = t * cs
        return t + pltpu.roll(t, QK_ROPE, 1)

    cqn = _rms(c_q, gq_ref[...]).astype(BF16)
    q = jnp.dot(cqn, wq_ref[...], preferred_element_type=F32) * q_scale
    for hd in range(N_HEADS):
        lo = hd * HEAD_PAD
        q_ref[:, lo:lo + QK_NOPE] = q[:, lo:lo + QK_NOPE].astype(BF16)
        q_ref[:, lo + QK_NOPE:lo + HEAD_PAD] = rotate(q[:, lo + QK_NOPE:lo + HEAD_PAD]).astype(BF16)

    ckvn = _rms(c_kv, gkv_ref[...]).astype(BF16)
    k_nope = jnp.dot(ckvn, wk_ref[...], preferred_element_type=F32)
    vt_ref[...] = lax.dot_general(wvt_ref[...], ckvn, (((1,), (1,)), ((), ())),
                                  preferred_element_type=F32).astype(BF16)
    lane = lax.broadcasted_iota(jnp.int32, k_rope.shape, 1)
    kr = jnp.where(lane < QK_ROPE, rotate(k_rope), 0.0).astype(BF16)
    for hd in range(N_HEADS):
        lo = hd * HEAD_PAD
        k_ref[:, lo:lo + QK_NOPE] = k_nope[:, hd * QK_NOPE:(hd + 1) * QK_NOPE].astype(BF16)
        k_ref[:, lo + QK_NOPE:lo + HEAD_PAD] = kr

    n_j = x.shape[0] // SSM_CHUNK
    for blk in range(n_blk):
        steps = [u_ref[blk, pl.ds(t, n_j, stride=SSM_CHUNK), :] for t in range(SSM_CHUNK)]
        groups = _block_transpose(steps)
        for gl in range(SUBLANES):
            uc_ref[blk * SUBLANES + gl] = groups[gl].astype(BF16)


def _in_proj(x2d, seq_len, w, cs):
    t, d = x2d.shape
    tm = _tile(seq_len, 512)
    n_blk = w["n_u"] // LANES
    n_g = w["n_u"] // SSM_GROUP
    kern = functools.partial(_in_proj_kernel, q_scale=w["q_scale"])
    tiles_per_seq = seq_len // tm
    row = lambda i: (i, 0)
    return pl.pallas_call(
        kern,
        grid=(t // tm,),
        in_specs=[
            pl.BlockSpec((tm, d), row),
            _const_spec((1, d)),
            _const_spec(w["w_in"].shape),
            _const_spec((1, Q_LORA)),
            _const_spec(w["w_q"].shape),
            _const_spec((1, KV_LORA)),
            _const_spec(w["w_k"].shape),
            _const_spec(w["w_vt"].shape),
            pl.BlockSpec((tm, LANES), lambda i: (i % tiles_per_seq, 0)),
        ],
        out_specs=[
            pl.BlockSpec((tm, N_HEADS * HEAD_PAD), row),
            pl.BlockSpec((tm, N_HEADS * HEAD_PAD), row),
            pl.BlockSpec((N_HEADS * V_HEAD, tm), lambda i: (0, i)),
            pl.BlockSpec((n_blk, tm, LANES), lambda i: (0, i, 0)),
            pl.BlockSpec((n_g, tm // SSM_CHUNK, LANES), lambda i: (0, i, 0)),
        ],
        out_shape=[
            jax.ShapeDtypeStruct((t, N_HEADS * HEAD_PAD), BF16),
            jax.ShapeDtypeStruct((t, N_HEADS * HEAD_PAD), BF16),
            jax.ShapeDtypeStruct((N_HEADS * V_HEAD, t), BF16),
            jax.ShapeDtypeStruct((n_blk, t, LANES), F32),
            jax.ShapeDtypeStruct((n_g, t // SSM_CHUNK, LANES), BF16),
        ],
        compiler_params=_params(("arbitrary",)),
        name="in_proj",
    )(x2d, w["g_pre_mix"], w["w_in"], w["g_q_a"], w["w_q"], w["g_kv_a"], w["w_k"], w["w_vt"], cs)


def _flash_kernel(q_ref, k_ref, vt_ref, o_ref, s_ref, mx_ref, m_ref, acc_ref, *, tk):
    n_k = k_ref.shape[1] // tk
    ones = jnp.ones((ACC_ROWS - V_HEAD, tk), BF16)

    def scores(i, slot):
        start = pl.multiple_of(i * tk, tk)
        s = lax.dot_general(k_ref[0, pl.ds(start, tk), :], q_ref[0],
                            (((1,), (1,)), ((), ())), preferred_element_type=F32)
        s_ref[slot] = s
        x = s.reshape(tk // SUBLANES, SUBLANES, s.shape[-1])
        while x.shape[0] > 1:
            half = x.shape[0] // 2
            x = jnp.maximum(x[:half], x[half:])
        mx_ref[slot] = x[0]

    def update(i, slot):
        m = m_ref[...]
        m_new = jnp.maximum(m, jnp.max(mx_ref[slot], axis=0, keepdims=True))
        alpha = jnp.exp2(m - m_new)
        p = jnp.exp2(s_ref[slot] - m_new).astype(BF16)
        start = pl.multiple_of(i * tk, tk)
        v_aug = jnp.concatenate([vt_ref[:, pl.ds(start, tk)], ones], axis=0)
        acc_ref[...] = alpha * acc_ref[...] + jnp.dot(v_aug, p, preferred_element_type=F32)
        m_ref[...] = m_new

    m_ref[...] = jnp.full_like(m_ref, -jnp.inf)
    acc_ref[...] = jnp.zeros_like(acc_ref)
    scores(0, 0)

    def body(j, _):
        scores(2 * j + 1, 1)
        update(2 * j, 0)
        scores(2 * j + 2, 0)
        update(2 * j + 1, 1)
        return 0

    lax.fori_loop(0, n_k // 2 - 1, body, 0)
    scores(n_k - 1, 1)
    update(n_k - 2, 0)
    update(n_k - 1, 1)
    acc = acc_ref[...]
    o_ref[0] = (acc[:V_HEAD] / acc[V_HEAD:V_HEAD + 1]).T.astype(o_ref.dtype)


def _attention(q, k, vt, batch, seq_len):
    q = q.reshape(batch, seq_len, N_HEADS * HEAD_PAD)
    k = k.reshape(batch, seq_len, N_HEADS * HEAD_PAD)
    tq = _tile(seq_len, 512)
    tk = _tile(seq_len // 2, 1024)
    out = pl.pallas_call(
        functools.partial(_flash_kernel, tk=tk),
        grid=(batch, N_HEADS, seq_len // tq),
        in_specs=[
            pl.BlockSpec((1, tq, HEAD_PAD), lambda b, h, i: (b, i, h)),
            pl.BlockSpec((1, seq_len, HEAD_PAD), lambda b, h, i: (b, 0, h)),
            pl.BlockSpec((V_HEAD, seq_len), lambda b, h, i: (h, b)),
        ],
        out_specs=pl.BlockSpec((1, tq, V_HEAD), lambda b, h, i: (b, i, h)),
        out_shape=jax.ShapeDtypeStruct((batch, seq_len, N_HEADS * V_HEAD), BF16),
        scratch_shapes=[pltpu.VMEM((2, tk, tq), F32), pltpu.VMEM((2, SUBLANES, tq), F32),
                        pltpu.VMEM((1, tq), F32), pltpu.VMEM((ACC_ROWS, tq), F32)],
        compiler_params=_params(("arbitrary", "arbitrary", "arbitrary")),
        name="attention",
    )(q, k, vt)
    return out.reshape(batch * seq_len, N_HEADS * V_HEAD)


def _ssm_matrices(lam_re, lam_im, log_dt, b_re, b_im, c_re, c_im):
    t_c = SSM_CHUNK
    n_g = lam_re.shape[1]
    w_in, w_state, a_r, a_s = [], [], [], []
    toep = 0.0
    for d in range(2):
        dt = jnp.exp(log_dt[d].astype(F32))[:, None]
        lam = lax.complex(lam_re[d].astype(F32), lam_im[d].astype(F32))
        a = jnp.exp(lam * dt)
        coef = (a - 1.0) / lam
        bt = lax.complex(b_re[d].astype(F32), b_im[d].astype(F32)) * coef[:, :, None]
        cc = lax.complex(c_re[d].astype(F32), c_im[d].astype(F32))
        steps = jnp.arange(t_c + 1, dtype=F32)
        apow = jnp.exp(lam[None] * dt[None] * steps[:, None, None])
        in_pow = apow[:t_c][::-1] if d == 0 else apow[:t_c]
        win = jnp.einsum("sgp,gpc->gscp", in_pow, bt).reshape(n_g, LANES, SSM_STATE)
        w_in.append(jnp.concatenate([win.real, win.imag, win.imag, win.real], axis=-1))
        out_pow = apow[1:] if d == 0 else apow[1:][::-1]
        wst = jnp.einsum("gcp,tgp->gptc", cc, out_pow).reshape(n_g, SSM_STATE, LANES)
        w_state.append(jnp.concatenate([wst.real, -wst.imag], axis=1))
        kern = jnp.einsum("gcp,kgp,gpe->kgce", cc, apow[:t_c], bt).real
        s_idx = jnp.arange(t_c)[:, None]
        t_idx = jnp.arange(t_c)[None, :]
        lag = (t_idx - s_idx) if d == 0 else (s_idx - t_idx)
        blocks = jnp.where((lag >= 0)[:, :, None, None, None], kern[jnp.clip(lag, 0, t_c - 1)], 0.0)
        toep = toep + blocks.transpose(2, 0, 4, 1, 3).reshape(n_g, LANES, LANES)
        a_t = apow[t_c]
        a_r.append(jnp.concatenate([a_t.real, a_t.real], axis=-1))
        a_s.append(jnp.concatenate([-a_t.imag, a_t.imag], axis=-1))
    w_out = jnp.concatenate([toep, w_state[0], w_state[1]], axis=1)
    return dict(w_in_f=w_in[0].astype(BF16), w_in_b=w_in[1].astype(BF16), w_out=w_out.astype(BF16),
                ar_f=a_r[0], as_f=a_s[0], ar_b=a_r[1], as_b=a_s[1])


def _ssm_state_kernel(uf_ref, ub_ref, wf_ref, wb_ref, arf_ref, asf_ref, arb_ref, asb_ref,
                      xf_ref, xb_ref, vf, vft, vb, vbt, carry, *, tj, pitch):
    n_g = uf_ref.shape[0]
    n_blk = n_g // SUBLANES

    @pl.when(pl.program_id(1) == 0)
    def _():
        carry[...] = jnp.zeros_like(carry)

    def project(blk, _):
        for gl in range(SUBLANES):
            g = blk * SUBLANES + gl
            row = pl.multiple_of(g * pitch, SUBLANES)
            pf = jnp.dot(uf_ref[g], wf_ref[g], preferred_element_type=F32)
            vf[pl.ds(row, tj), :] = pf[:, :LANES]
            vft[pl.ds(row, tj), :] = pf[:, LANES:]
            pb = jnp.dot(ub_ref[g], wb_ref[g], preferred_element_type=F32)
            vb[pl.ds(row, tj), :] = pb[:, :LANES]
            vbt[pl.ds(row, tj), :] = pb[:, LANES:]
        return 0

    lax.fori_loop(0, n_blk, project, 0)

    def scan(j, state):
        jb = tj - 1 - j
        new = []
        for blk in range(n_blk):
            base = blk * SUBLANES * pitch
            gs = pl.ds(blk * SUBLANES, SUBLANES)
            for d, (v, vt, ar_ref, as_ref, jj) in enumerate(
                    ((vf, vft, arf_ref, asf_ref, j), (vb, vbt, arb_ref, asb_ref, jb))):
                x, xt = state[(blk * 2 + d) * 2], state[(blk * 2 + d) * 2 + 1]
                rows = pl.ds(base + jj, SUBLANES, stride=pitch)
                vx = v[rows, :]
                vxt = vt[rows, :]
                v[rows, :] = x
                ar = ar_ref[gs, :]
                a_s = as_ref[gs, :]
                new.append(ar * x + a_s * xt + vx)
                new.append(ar * xt - a_s * x + vxt)
        return tuple(new)

    init = tuple(carry[i] for i in range(4 * n_blk))
    final = lax.fori_loop(0, tj, scan, init)
    for i in range(4 * n_blk):
        carry[i] = final[i]

    def emit(g, _):
        row = pl.multiple_of(g * pitch, SUBLANES)
        xf_ref[g] = vf[pl.ds(row, tj), :].astype(BF16)
        xb_ref[g] = vb[pl.ds(row, tj), :].astype(BF16)
        return 0

    lax.fori_loop(0, n_g, emit, 0, unroll=4)


def _ssm_out_kernel(u_ref, xf_ref, xb_ref, w_ref, y_ref, *, tj):
    n_blk = y_ref.shape[0]

    def body(blk, _):
        groups = []
        for gl in range(SUBLANES):
            g = blk * SUBLANES + gl
            lhs = jnp.concatenate([u_ref[g], xf_ref[g], xb_ref[g]], axis=1)
            groups.append(jnp.dot(lhs, w_ref[g], preferred_element_type=F32))
        steps = _block_transpose(groups)
        for t in range(SSM_CHUNK):
            y_ref[blk, pl.ds(t, tj, stride=SSM_CHUNK), :] = steps[t]
        return 0

    lax.fori_loop(0, n_blk, body, 0)


def _ssm(uc, batch, seq_len, w):
    n_g = uc.shape[0]
    n_j = seq_len // SSM_CHUNK
    tj = _tile(n_j, 128)
    n_t = n_j // tj
    pitch = tj + SSM_PITCH_PAD
    blk = (n_g, tj, LANES)
    fwd = lambda b, i: (0, b * n_t + i, 0)
    bwd = lambda b, i: (0, b * n_t + n_t - 1 - i, 0)
    scan_buf = pltpu.VMEM((n_g * pitch, LANES), F32)
    xf, xb = pl.pallas_call(
        functools.partial(_ssm_state_kernel, tj=tj, pitch=pitch),
        grid=(batch, n_t),
        in_specs=[
            pl.BlockSpec(blk, fwd),
            pl.BlockSpec(blk, bwd),
            _const_spec(w["w_in_f"].shape),
            _const_spec(w["w_in_b"].shape),
            _const_spec(w["ar_f"].shape),
            _const_spec(w["as_f"].shape),
            _const_spec(w["ar_b"].shape),
            _const_spec(w["as_b"].shape),
        ],
        out_specs=[pl.BlockSpec(blk, fwd), pl.BlockSpec(blk, bwd)],
        out_shape=[jax.ShapeDtypeStruct(uc.shape, BF16)] * 2,
        scratch_shapes=[scan_buf, scan_buf, scan_buf, scan_buf,
                        pltpu.VMEM((4 * (n_g // SUBLANES), SUBLANES, LANES), F32)],
        compiler_params=_params(("arbitrary", "arbitrary")),
        name="ssm_state",
    )(uc, uc, w["w_in_f"], w["w_in_b"], w["ar_f"], w["as_f"], w["ar_b"], w["as_b"])
    n_blk = n_g // SUBLANES
    return pl.pallas_call(
        functools.partial(_ssm_out_kernel, tj=tj),
        grid=(batch, n_t),
        in_specs=[pl.BlockSpec(blk, fwd), pl.BlockSpec(blk, fwd), pl.BlockSpec(blk, fwd),
                  _const_spec(w["w_out"].shape)],
        out_specs=pl.BlockSpec((n_blk, tj * SSM_CHUNK, LANES), fwd),
        out_shape=jax.ShapeDtypeStruct((n_blk, batch * seq_len, LANES), F32),
        compiler_params=_params(("arbitrary", "arbitrary")),
        name="ssm_out",
    )(uc, xf, xb, w["w_out"])


def _mix_out_kernel(ys_ref, u_ref, a_ref, x_ref, d_ref, wglu_ref, bglu_ref, ga_ref, gs_ref,
                    woa_ref, wos_ref, gpost_ref, gffn_ref, x1_ref, h2_ref):
    n_blk = ys_ref.shape[0]
    ys = jnp.concatenate([ys_ref[b] for b in range(n_blk)], axis=1)
    u = jnp.concatenate([u_ref[b] for b in range(n_blk)], axis=1)
    y = jax.nn.gelu(ys + d_ref[...] * u)
    gate = jax.nn.sigmoid(jnp.dot(y.astype(BF16), wglu_ref[...], preferred_element_type=F32)
                          + bglu_ref[...])
    s = y * gate
    an = _rms(a_ref[...].astype(F32), ga_ref[...]).astype(BF16)
    sn = _rms(s, gs_ref[...]).astype(BF16)
    m = (jnp.dot(an, woa_ref[...], preferred_element_type=F32)
         + jnp.dot(sn, wos_ref[...], preferred_element_type=F32))
    x1 = x_ref[...] + _rms(m, gpost_ref[...])
    x1_ref[...] = x1
    h2_ref[...] = _rms(x1, gffn_ref[...]).astype(BF16)


def _mix_out(ys, u, a, x2d, w):
    t, d = x2d.shape
    n_a = a.shape[1]
    n_blk = ys.shape[0]
    n_s = n_blk * LANES
    tm = _tile(t, 256)
    row = lambda i: (i, 0)
    return pl.pallas_call(
        _mix_out_kernel,
        grid=(t // tm,),
        in_specs=[
            pl.BlockSpec((n_blk, tm, LANES), lambda i: (0, i, 0)),
            pl.BlockSpec((n_blk, tm, LANES), lambda i: (0, i, 0)),
            pl.BlockSpec((tm, n_a), row),
            pl.BlockSpec((tm, d), row),
            _const_spec((1, n_s)),
            _const_spec(w["w_glu"].shape),
            _const_spec((1, n_s)),
            _const_spec((1, n_a)),
            _const_spec((1, n_s)),
            _const_spec(w["w_out_a"].shape),
            _const_spec(w["w_out_s"].shape),
            _const_spec((1, d)),
            _const_spec((1, d)),
        ],
        out_specs=[pl.BlockSpec((tm, d), row), pl.BlockSpec((tm, d), row)],
        out_shape=[jax.ShapeDtypeStruct((t, d), F32), jax.ShapeDtypeStruct((t, d), BF16)],
        compiler_params=_params(("arbitrary",)),
        name="mix_out",
    )(ys, u, a, x2d, w["ssm_d"], w["w_glu"], w["b_glu"], w["g_out_attn"], w["g_out_ssm"],
      w["w_out_a"], w["w_out_s"], w["g_post_mix"], w["g_pre_ffn"])


def _ffn_kernel(h_ref, x1_ref, wg_ref, wu_ref, wd_ref, gpost_ref, o_ref, acc_ref):
    j = pl.program_id(1)

    @pl.when(j == 0)
    def _():
        acc_ref[...] = jnp.zeros_like(acc_ref)

    h = h_ref[...]
    gate = jnp.dot(h, wg_ref[...], preferred_element_type=F32)
    up = jnp.dot(h, wu_ref[...], preferred_element_type=F32)
    act = (jax.nn.silu(gate) * up).astype(BF16)
    acc_ref[...] += jnp.dot(act, wd_ref[...], preferred_element_type=F32)

    @pl.when(j == pl.num_programs(1) - 1)
    def _():
        o_ref[...] = x1_ref[...] + _rms(acc_ref[...], gpost_ref[...])


def _ffn(h2, x1, w):
    t, d = x1.shape
    d_ff = w["w_gate"].shape[1]
    tm = _tile(t, 512)
    tf = _tile(d_ff, 512)
    return pl.pallas_call(
        _ffn_kernel,
        grid=(t // tm, d_ff // tf),
        in_specs=[
            pl.BlockSpec((tm, d), lambda i, j: (i, 0)),
            pl.BlockSpec((tm, d), lambda i, j: (i, 0)),
            pl.BlockSpec((d, tf), lambda i, j: (0, j)),
            pl.BlockSpec((d, tf), lambda i, j: (0, j)),
            pl.BlockSpec((tf, d), lambda i, j: (j, 0)),
            pl.BlockSpec((1, d), lambda i, j: (0, 0)),
        ],
        out_specs=pl.BlockSpec((tm, d), lambda i, j: (i, 0)),
        out_shape=jax.ShapeDtypeStruct((t, d), F32),
        scratch_shapes=[pltpu.VMEM((tm, d), F32)],
        compiler_params=_params(("arbitrary", "arbitrary")),
        name="ffn",
    )(h2, x1, w["w_gate"], w["w_up"], w["w_down"], w["g_post_ffn"])


def _rope_table(length):
    inv = 1.0 / (ROPE_THETA ** (jnp.arange(0, QK_ROPE, 2, dtype=F32) / QK_ROPE))
    ang = jnp.arange(length, dtype=F32)[:, None] * inv[None, :]
    cos, sin = jnp.cos(ang), jnp.sin(ang)
    return jnp.concatenate([cos, cos, -sin, sin], axis=-1)


def _swap_halves(wr):
    half = QK_ROPE // 2
    return jnp.concatenate([wr[..., half:], wr[..., :half]], axis=-1)


def _prepare_weights(g_pre_mix, w_in, g_q_a, w_q_b, g_kv_a, w_kv_b, lam_re, lam_im, log_dt,
                     b_re, b_im, c_re, c_im, ssm_d, w_glu, b_glu, g_out_attn, g_out_ssm, w_out,
                     g_post_mix, g_pre_ffn, w_gate, w_up, w_down, g_post_ffn):
    row = lambda g: g.astype(F32).reshape(1, -1)
    i1 = Q_LORA + KV_LORA
    i2 = i1 + QK_ROPE
    w_rope = w_in[:, i1:i2]
    w_in_ext = jnp.concatenate([w_in[:, :i1], w_in[:, i2:], w_rope, _swap_halves(w_rope)], axis=1)
    wq = w_q_b.reshape(Q_LORA, N_HEADS, QK_NOPE + QK_ROPE)
    wq_ext = jnp.concatenate([wq, _swap_halves(wq[..., QK_NOPE:])], axis=-1)
    wkv = w_kv_b.reshape(KV_LORA, N_HEADS, QK_NOPE + V_HEAD)
    n_a = N_HEADS * V_HEAD
    out = dict(
        n_u=w_in.shape[1] - i2,
        q_scale=float((QK_NOPE + QK_ROPE) ** -0.5 * math.log2(math.e)),
        g_pre_mix=row(g_pre_mix), w_in=w_in_ext.astype(BF16),
        g_q_a=row(g_q_a), w_q=wq_ext.reshape(Q_LORA, N_HEADS * HEAD_PAD).astype(BF16),
        g_kv_a=row(g_kv_a),
        w_k=wkv[..., :QK_NOPE].reshape(KV_LORA, N_HEADS * QK_NOPE).astype(BF16),
        w_vt=wkv[..., QK_NOPE:].reshape(KV_LORA, N_HEADS * V_HEAD).T.astype(BF16),
        ssm_d=row(ssm_d), w_glu=w_glu.astype(BF16), b_glu=row(b_glu),
        g_out_attn=row(g_out_attn), g_out_ssm=row(g_out_ssm),
        w_out_a=w_out[:n_a].astype(BF16), w_out_s=w_out[n_a:].astype(BF16),
        g_post_mix=row(g_post_mix), g_pre_ffn=row(g_pre_ffn),
        w_gate=w_gate.astype(BF16), w_up=w_up.astype(BF16), w_down=w_down.astype(BF16),
        g_post_ffn=row(g_post_ffn),
    )
    out.update(_ssm_matrices(lam_re, lam_im, log_dt, b_re, b_im, c_re, c_im))
    return out


def _layer(x, w):
    batch, seq_len, d = x.shape
    x2d = x.reshape(batch * seq_len, d)
    q, k, vt, u, uc = _in_proj(x2d, seq_len, w, _rope_table(seq_len))
    a = _attention(q, k, vt, batch, seq_len)
    ys = _ssm(uc, batch, seq_len, w)
    x1, h2 = _mix_out(ys, u, a, x2d, w)
    return _ffn(h2, x1, w).reshape(batch, seq_len, d)


def kernel(x_prompt, x_sample, g_pre_mix, w_in, g_q_a, w_q_b, g_kv_a, w_kv_b, ssm_lam_re, ssm_lam_im, ssm_log_dt, ssm_b_re, ssm_b_im, ssm_c_re, ssm_c_im, ssm_d, w_glu, b_glu, g_out_attn, g_out_ssm, w_out, g_post_mix, g_pre_ffn, w_gate, w_up, w_down, g_post_ffn):
    weights = (g_pre_mix, w_in, g_q_a, w_q_b, g_kv_a, w_kv_b, ssm_lam_re, ssm_lam_im, ssm_log_dt,
               ssm_b_re, ssm_b_im, ssm_c_re, ssm_c_im, ssm_d, w_glu, b_glu, g_out_attn, g_out_ssm,
               w_out, g_post_mix, g_pre_ffn, w_gate, w_up, w_down, g_post_ffn)
    depth = g_pre_mix.shape[0]
    for layer in range(depth):
        w = _prepare_weights(*[p[layer] for p in weights])
        x_prompt = _layer(x_prompt, w)
        x_sample = _layer(x_sample, w)
    return (x_prompt, x_sample)
```

```python
import functools
import math

import jax
import jax.numpy as jnp
from jax import lax
from jax.experimental import pallas as pl
from jax.experimental.pallas import tpu as pltpu

F32 = jnp.float32
BF16 = jnp.bfloat16

RMS_EPS = 1e-6
ROPE_THETA = 10000.0
LANES = 128
SUBLANES = 8
VMEM_LIMIT_BYTES = 56 * 1024 * 1024

N_HEADS = 8
QK_NOPE = 128
QK_ROPE = 64
V_HEAD = 128
HEAD_PAD = 256
Q_LORA = 512
KV_LORA = 256
ACC_ROWS = V_HEAD + 16
ATTN_UNROLL = 4
SSM_GROUP = 16
SSM_STATE = 64
SSM_CHUNK = LANES // SSM_GROUP
SSM_PITCH_PAD = 8


def _rms(x, g):
    return x * lax.rsqrt(jnp.mean(x * x, axis=-1, keepdims=True) + RMS_EPS) * g


def _params(sem):
    return pltpu.CompilerParams(dimension_semantics=sem, vmem_limit_bytes=VMEM_LIMIT_BYTES)


def _tile(n, target):
    t = min(n, target)
    while n % t:
        t //= 2
    return t


def _block_transpose(a):
    lane = lax.broadcasted_iota(jnp.int32, a[0].shape, 1)
    for d in (4, 2, 1):
        shift = SSM_GROUP * d
        upper = (lane & shift) != 0
        new = list(a)
        for i in range(len(a)):
            if i & d:
                continue
            x, y = a[i], a[i | d]
            new[i] = jnp.where(upper, pltpu.roll(y, shift, 1), x)
            new[i | d] = jnp.where(upper, y, pltpu.roll(x, LANES - shift, 1))
        a = new
    return a


def _const_spec(shape):
    nd = len(shape)
    return pl.BlockSpec(shape, lambda *_: (0,) * nd, pipeline_mode=pl.Buffered(1))


def _in_proj_kernel(x_ref, gpre_ref, win_ref, gq_ref, wq_ref, gkv_ref, wk_ref, wvt_ref, cs_ref,
                    q_ref, k_ref, vt_ref, u_ref, uc_ref, *, q_scale):
    x = x_ref[...]
    h = _rms(x, gpre_ref[...])
    z = jnp.dot(h.astype(BF16), win_ref[...], preferred_element_type=F32)
    c_q = z[:, :Q_LORA]
    c_kv = z[:, Q_LORA:Q_LORA + KV_LORA]
    n_blk = u_ref.shape[0]
    u0 = Q_LORA + KV_LORA
    for blk in range(n_blk):
        u_ref[blk] = z[:, u0 + blk * LANES:u0 + (blk + 1) * LANES]
    k_rope = z[:, u0 + n_blk * LANES:]
    cs = cs_ref[...]

    def rotate(t):
        t = t * cs
        return t + pltpu.roll(t, QK_ROPE, 1)

    cqn = _rms(c_q, gq_ref[...]).astype(BF16)
    q = jnp.dot(cqn, wq_ref[...], preferred_element_type=F32) * q_scale
    for hd in range(N_HEADS):
        lo = hd * HEAD_PAD
        q_ref[:, lo:lo + QK_NOPE] = q[:, lo:lo + QK_NOPE].astype(BF16)
        q_ref[:, lo + QK_NOPE:lo + HEAD_PAD] = rotate(q[:, lo + QK_NOPE:lo + HEAD_PAD]).astype(BF16)

    ckvn = _rms(c_kv, gkv_ref[...]).astype(BF16)
    k_nope = jnp.dot(ckvn, wk_ref[...], preferred_element_type=F32)
    vt_ref[...] = lax.dot_general(wvt_ref[...], ckvn, (((1,), (1,)), ((), ())),
                                  preferred_element_type=F32).astype(BF16)
    lane = lax.broadcasted_iota(jnp.int32, k_rope.shape, 1)
    kr = jnp.where(lane < QK_ROPE, rotate(k_rope), 0.0).astype(BF16)
    for hd in range(N_HEADS):
        lo = hd * HEAD_PAD
        k_ref[:, lo:lo + QK_NOPE] = k_nope[:, hd * QK_NOPE:(hd + 1) * QK_NOPE].astype(BF16)
        k_ref[:, lo + QK_NOPE:lo + HEAD_PAD] = kr

    n_j = x.shape[0] // SSM_CHUNK
    for blk in range(n_blk):
        steps = [u_ref[blk, pl.ds(t, n_j, stride=SSM_CHUNK), :] for t in range(SSM_CHUNK)]
        groups = _block_transpose(steps)
        for gl in range(SUBLANES):
            uc_ref[blk * SUBLANES + gl] = groups[gl].astype(BF16)


def _in_proj(x2d, seq_len, w, cs):
    t, d = x2d.shape
    tm = _tile(seq_len, 512)
    n_blk = w["n_u"] // LANES
    n_g = w["n_u"] // SSM_GROUP
    kern = functools.partial(_in_proj_kernel, q_scale=w["q_scale"])
    tiles_per_seq = seq_len // tm
    row = lambda i: (i, 0)
    return pl.pallas_call(
        kern,
        grid=(t // tm,),
        in_specs=[
            pl.BlockSpec((tm, d), row),
            _const_spec((1, d)),
            _const_spec(w["w_in"].shape),
            _const_spec((1, Q_LORA)),
            _const_spec(w["w_q"].shape),
            _const_spec((1, KV_LORA)),
            _const_spec(w["w_k"].shape),
            _const_spec(w["w_vt"].shape),
            pl.BlockSpec((tm, LANES), lambda i: (i % tiles_per_seq, 0)),
        ],
        out_specs=[
            pl.BlockSpec((tm, N_HEADS * HEAD_PAD), row),
            pl.BlockSpec((tm, N_HEADS * HEAD_PAD), row),
            pl.BlockSpec((N_HEADS * V_HEAD, tm), lambda i: (0, i)),
            pl.BlockSpec((n_blk, tm, LANES), lambda i: (0, i, 0)),
            pl.BlockSpec((n_g, tm // SSM_CHUNK, LANES), lambda i: (0, i, 0)),
        ],
        out_shape=[
            jax.ShapeDtypeStruct((t, N_HEADS * HEAD_PAD), BF16),
            jax.ShapeDtypeStruct((t, N_HEADS * HEAD_PAD), BF16),
            jax.ShapeDtypeStruct((N_HEADS * V_HEAD, t), BF16),
            jax.ShapeDtypeStruct((n_blk, t, LANES), F32),
            jax.ShapeDtypeStruct((n_g, t // SSM_CHUNK, LANES), BF16),
        ],
        compiler_params=_params(("arbitrary",)),
        name="in_proj",
    )(x2d, w["g_pre_mix"], w["w_in"], w["g_q_a"], w["w_q"], w["g_kv_a"], w["w_k"], w["w_vt"], cs)


def _flash_kernel(q_ref, k_ref, vt_ref, o_ref, s_ref, mx_ref, m_ref, acc_ref, *, tk):
    n_k = k_ref.shape[1] // tk
    ones = jnp.ones((ACC_ROWS - V_HEAD, tk), BF16)

    def scores(i, slot):
        start = pl.multiple_of(i * tk, tk)
        s = lax.dot_general(k_ref[0, pl.ds(start, tk), :], q_ref[0],
                            (((1,), (1,)), ((), ())), preferred_element_type=F32)
        s_ref[slot] = s
        x = s.reshape(tk // SUBLANES, SUBLANES, s.shape[-1])
        while x.shape[0] > 1:
            half = x.shape[0] // 2
            x = jnp.maximum(x[:half], x[half:])
        mx_ref[slot] = x[0]

    def update(i, slot):
        m = m_ref[...]
        m_new = jnp.maximum(m, jnp.max(mx_ref[slot], axis=0, keepdims=True))
        alpha = jnp.exp2(m - m_new)
        p = jnp.exp2(s_ref[slot] - m_new).astype(BF16)
        start = pl.multiple_of(i * tk, tk)
        v_aug = jnp.concatenate([vt_ref[:, pl.ds(start, tk)], ones], axis=0)
        acc_ref[...] = alpha * acc_ref[...] + jnp.dot(v_aug, p, preferred_element_type=F32)
        m_ref[...] = m_new

    m_ref[...] = jnp.full_like(m_ref, -jnp.inf)
    acc_ref[...] = jnp.zeros_like(acc_ref)
    scores(0, 0)

    unroll = ATTN_UNROLL if n_k >= 2 * ATTN_UNROLL else 2

    def body(j, _):
        for t in range(unroll):
            scores(j * unroll + t + 1, (t + 1) % 2)
            update(j * unroll + t, t % 2)
        return 0

    n_loop = (n_k - 1) // unroll
    lax.fori_loop(0, n_loop, body, 0)
    for i in range(n_loop * unroll, n_k):
        if i + 1 < n_k:
            scores(i + 1, (i + 1) % 2)
        update(i, i % 2)
    acc = acc_ref[...]
    o_ref[0] = (acc[:V_HEAD] / acc[V_HEAD:V_HEAD + 1]).T.astype(o_ref.dtype)


def _attention(q, k, vt, batch, seq_len):
    q = q.reshape(batch, seq_len, N_HEADS * HEAD_PAD)
    k = k.reshape(batch, seq_len, N_HEADS * HEAD_PAD)
    tq = _tile(seq_len, 512)
    tk = _tile(seq_len, 1024)
    out = pl.pallas_call(
        functools.partial(_flash_kernel, tk=tk),
        grid=(batch, N_HEADS, seq_len // tq),
        in_specs=[
            pl.BlockSpec((1, tq, HEAD_PAD), lambda b, h, i: (b, i, h)),
            pl.BlockSpec((1, seq_len, HEAD_PAD), lambda b, h, i: (b, 0, h)),
            pl.BlockSpec((V_HEAD, seq_len), lambda b, h, i: (h, b)),
        ],
        out_specs=pl.BlockSpec((1, tq, V_HEAD), lambda b, h, i: (b, i, h)),
        out_shape=jax.ShapeDtypeStruct((batch, seq_len, N_HEADS * V_HEAD), BF16),
        scratch_shapes=[pltpu.VMEM((2, tk, tq), F32), pltpu.VMEM((2, SUBLANES, tq), F32),
                        pltpu.VMEM((1, tq), F32), pltpu.VMEM((ACC_ROWS, tq), F32)],
        compiler_params=_params(("arbitrary", "arbitrary", "arbitrary")),
        name="attention",
    )(q, k, vt)
    return out.reshape(batch * seq_len, N_HEADS * V_HEAD)


def _ssm_matrices(lam_re, lam_im, log_dt, b_re, b_im, c_re, c_im):
    t_c = SSM_CHUNK
    n_g = lam_re.shape[1]
    w_in, w_state, a_r, a_s = [], [], [], []
    toep = 0.0
    for d in range(2):
        dt = jnp.exp(log_dt[d].astype(F32))[:, None]
        lam = lax.complex(lam_re[d].astype(F32), lam_im[d].astype(F32))
        a = jnp.exp(lam * dt)
        coef = (a - 1.0) / lam
        bt = lax.complex(b_re[d].astype(F32), b_im[d].astype(F32)) * coef[:, :, None]
        cc = lax.complex(c_re[d].astype(F32), c_im[d].astype(F32))
        steps = jnp.arange(t_c + 1, dtype=F32)
        apow = jnp.exp(lam[None] * dt[None] * steps[:, None, None])
        in_pow = apow[:t_c][::-1] if d == 0 else apow[:t_c]
        win = jnp.einsum("sgp,gpc->gscp", in_pow, bt).reshape(n_g, LANES, SSM_STATE)
        w_in.append(jnp.concatenate([win.real, win.imag, win.imag, win.real], axis=-1))
        out_pow = apow[1:] if d == 0 else apow[1:][::-1]
        wst = jnp.einsum("gcp,tgp->gptc", cc, out_pow).reshape(n_g, SSM_STATE, LANES)
        w_state.append(jnp.concatenate([wst.real, -wst.imag], axis=1))
        kern = jnp.einsum("gcp,kgp,gpe->kgce", cc, apow[:t_c], bt).real
        if d == 0:
            rows = [jnp.concatenate([jnp.zeros_like(kern[:s]), kern[:t_c - s]]) for s in range(t_c)]
        else:
            rows = [jnp.concatenate([kern[:s + 1][::-1], jnp.zeros_like(kern[:t_c - 1 - s])])
                    for s in range(t_c)]
        blocks = jnp.stack(rows)
        toep = toep + blocks.transpose(2, 0, 4, 1, 3).reshape(n_g, LANES, LANES)
        a_t = apow[t_c]
        a_r.append(jnp.concatenate([a_t.real, a_t.real], axis=-1))
        a_s.append(jnp.concatenate([-a_t.imag, a_t.imag], axis=-1))
    w_out = jnp.concatenate([toep, w_state[0], w_state[1]], axis=1)
    return dict(w_in_f=w_in[0].astype(BF16), w_in_b=w_in[1].astype(BF16), w_out=w_out.astype(BF16),
                ar_f=a_r[0], as_f=a_s[0], ar_b=a_r[1], as_b=a_s[1])


def _ssm_state_kernel(uf_ref, ub_ref, wf_ref, wb_ref, arf_ref, asf_ref, arb_ref, asb_ref,
                      xf_ref, xb_ref, vf, vft, vb, vbt, carry, *, tj, pitch):
    n_g = uf_ref.shape[0]
    n_blk = n_g // SUBLANES

    @pl.when(pl.program_id(1) == 0)
    def _():
        carry[...] = jnp.zeros_like(carry)

    def project(blk, _):
        for gl in range(SUBLANES):
            g = blk * SUBLANES + gl
            row = pl.multiple_of(g * pitch, SUBLANES)
            pf = jnp.dot(uf_ref[g], wf_ref[g], preferred_element_type=F32)
            vf[pl.ds(row, tj), :] = pf[:, :LANES]
            vft[pl.ds(row, tj), :] = pf[:, LANES:]
            pb = jnp.dot(ub_ref[g], wb_ref[g], preferred_element_type=F32)
            vb[pl.ds(row, tj), :] = pb[:, :LANES]
            vbt[pl.ds(row, tj), :] = pb[:, LANES:]
        return 0

    lax.fori_loop(0, n_blk, project, 0)

    def scan(j, state):
        jb = tj - 1 - j
        new = []
        for blk in range(n_blk):
            base = blk * SUBLANES * pitch
            gs = pl.ds(blk * SUBLANES, SUBLANES)
            for d, (v, vt, ar_ref, as_ref, jj) in enumerate(
                    ((vf, vft, arf_ref, asf_ref, j), (vb, vbt, arb_ref, asb_ref, jb))):
                x, xt = state[(blk * 2 + d) * 2], state[(blk * 2 + d) * 2 + 1]
                rows = pl.ds(base + jj, SUBLANES, stride=pitch)
                vx = v[rows, :]
                vxt = vt[rows, :]
                v[rows, :] = x
                ar = ar_ref[gs, :]
                a_s = as_ref[gs, :]
                new.append(ar * x + a_s * xt + vx)
                new.append(ar * xt - a_s * x + vxt)
        return tuple(new)

    init = tuple(carry[i] for i in range(4 * n_blk))
    final = lax.fori_loop(0, tj, scan, init)
    for i in range(4 * n_blk):
        carry[i] = final[i]

    def emit(g, _):
        row = pl.multiple_of(g * pitch, SUBLANES)
        xf_ref[g] = vf[pl.ds(row, tj), :].astype(BF16)
        xb_ref[g] = vb[pl.ds(row, tj), :].astype(BF16)
        return 0

    lax.fori_loop(0, n_g, emit, 0, unroll=4)


def _ssm_out_kernel(u_ref, xf_ref, xb_ref, w_ref, y_ref, *, tj):
    n_blk = y_ref.shape[0]

    def body(blk, _):
        groups = []
        for gl in range(SUBLANES):
            g = blk * SUBLANES + gl
            lhs = jnp.concatenate([u_ref[g], xf_ref[g], xb_ref[g]], axis=1)
            groups.append(jnp.dot(lhs, w_ref[g], preferred_element_type=F32))
        steps = _block_transpose(groups)
        for t in range(SSM_CHUNK):
            y_ref[blk, pl.ds(t, tj, stride=SSM_CHUNK), :] = steps[t]
        return 0

    lax.fori_loop(0, n_blk, body, 0)


def _ssm(uc, batch, seq_len, w):
    n_g = uc.shape[0]
    n_j = seq_len // SSM_CHUNK
    tj = _tile(n_j, 128)
    n_t = n_j // tj
    pitch = tj + SSM_PITCH_PAD
    blk = (n_g, tj, LANES)
    fwd = lambda b, i: (0, b * n_t + i, 0)
    bwd = lambda b, i: (0, b * n_t + n_t - 1 - i, 0)
    scan_buf = pltpu.VMEM((n_g * pitch, LANES), F32)
    xf, xb = pl.pallas_call(
        functools.partial(_ssm_state_kernel, tj=tj, pitch=pitch),
        grid=(batch, n_t),
        in_specs=[
            pl.BlockSpec(blk, fwd),
            pl.BlockSpec(blk, bwd),
            _const_spec(w["w_in_f"].shape),
            _const_spec(w["w_in_b"].shape),
            _const_spec(w["ar_f"].shape),
            _const_spec(w["as_f"].shape),
            _const_spec(w["ar_b"].shape),
            _const_spec(w["as_b"].shape),
        ],
        out_specs=[pl.BlockSpec(blk, fwd), pl.BlockSpec(blk, bwd)],
        out_shape=[jax.ShapeDtypeStruct(uc.shape, BF16)] * 2,
        scratch_shapes=[scan_buf, scan_buf, scan_buf, scan_buf,
                        pltpu.VMEM((4 * (n_g // SUBLANES), SUBLANES, LANES), F32)],
        compiler_params=_params(("arbitrary", "arbitrary")),
        name="ssm_state",
    )(uc, uc, w["w_in_f"], w["w_in_b"], w["ar_f"], w["as_f"], w["ar_b"], w["as_b"])
    n_blk = n_g // SUBLANES
    return pl.pallas_call(
        functools.partial(_ssm_out_kernel, tj=tj),
        grid=(batch, n_t),
        in_specs=[pl.BlockSpec(blk, fwd), pl.BlockSpec(blk, fwd), pl.BlockSpec(blk, fwd),
                  _const_spec(w["w_out"].shape)],
        out_specs=pl.BlockSpec((n_blk, tj * SSM_CHUNK, LANES), fwd),
        out_shape=jax.ShapeDtypeStruct((n_blk, batch * seq_len, LANES), F32),
        compiler_params=_params(("arbitrary", "arbitrary")),
        name="ssm_out",
    )(uc, xf, xb, w["w_out"])


def _mix_out_kernel(ys_ref, u_ref, a_ref, x_ref, d_ref, wglu_ref, bglu_ref, ga_ref, gs_ref,
                    woa_ref, wos_ref, gpost_ref, gffn_ref, x1_ref, h2_ref):
    n_blk = ys_ref.shape[0]
    ys = jnp.concatenate([ys_ref[b] for b in range(n_blk)], axis=1)
    u = jnp.concatenate([u_ref[b] for b in range(n_blk)], axis=1)
    y = jax.nn.gelu(ys + d_ref[...] * u)
    gate = jax.nn.sigmoid(jnp.dot(y.astype(BF16), wglu_ref[...], preferred_element_type=F32)
                          + bglu_ref[...])
    s = y * gate
    an = _rms(a_ref[...].astype(F32), ga_ref[...]).astype(BF16)
    sn = _rms(s, gs_ref[...]).astype(BF16)
    m = (jnp.dot(an, woa_ref[...], preferred_element_type=F32)
         + jnp.dot(sn, wos_ref[...], preferred_element_type=F32))
    x1 = x_ref[...] + _rms(m, gpost_ref[...])
    x1_ref[...] = x1
    h2_ref[...] = _rms(x1, gffn_ref[...]).astype(BF16)


def _mix_out(ys, u, a, x2d, w):
    t, d = x2d.shape
    n_a = a.shape[1]
    n_blk = ys.shape[0]
    n_s = n_blk * LANES
    tm = _tile(t, 512)
    row = lambda i: (i, 0)
    return pl.pallas_call(
        _mix_out_kernel,
        grid=(t // tm,),
        in_specs=[
            pl.BlockSpec((n_blk, tm, LANES), lambda i: (0, i, 0)),
            pl.BlockSpec((n_blk, tm, LANES), lambda i: (0, i, 0)),
            pl.BlockSpec((tm, n_a), row),
            pl.BlockSpec((tm, d), row),
            _const_spec((1, n_s)),
            _const_spec(w["w_glu"].shape),
            _const_spec((1, n_s)),
            _const_spec((1, n_a)),
            _const_spec((1, n_s)),
            _const_spec(w["w_out_a"].shape),
            _const_spec(w["w_out_s"].shape),
            _const_spec((1, d)),
            _const_spec((1, d)),
        ],
        out_specs=[pl.BlockSpec((tm, d), row), pl.BlockSpec((tm, d), row)],
        out_shape=[jax.ShapeDtypeStruct((t, d), F32), jax.ShapeDtypeStruct((t, d), BF16)],
        compiler_params=_params(("arbitrary",)),
        name="mix_out",
    )(ys, u, a, x2d, w["ssm_d"], w["w_glu"], w["b_glu"], w["g_out_attn"], w["g_out_ssm"],
      w["w_out_a"], w["w_out_s"], w["g_post_mix"], w["g_pre_ffn"])


def _ffn_kernel(h_ref, x1_ref, wg_ref, wu_ref, wd_ref, gpost_ref, o_ref, acc_ref):
    j = pl.program_id(1)

    @pl.when(j == 0)
    def _():
        acc_ref[...] = jnp.zeros_like(acc_ref)

    h = h_ref[...]
    gate = jnp.dot(h, wg_ref[...], preferred_element_type=F32)
    up = jnp.dot(h, wu_ref[...], preferred_element_type=F32)
    act = (jax.nn.silu(gate) * up).astype(BF16)
    acc_ref[...] += jnp.dot(act, wd_ref[...], preferred_element_type=F32)

    @pl.when(j == pl.num_programs(1) - 1)
    def _():
        o_ref[...] = x1_ref[...] + _rms(acc_ref[...], gpost_ref[...])


def _ffn(h2, x1, w):
    t, d = x1.shape
    d_ff = w["w_gate"].shape[1]
    tm = _tile(t, 512)
    tf = _tile(d_ff, 512)
    return pl.pallas_call(
        _ffn_kernel,
        grid=(t // tm, d_ff // tf),
        in_specs=[
            pl.BlockSpec((tm, d), lambda i, j: (i, 0)),
            pl.BlockSpec((tm, d), lambda i, j: (i, 0)),
            pl.BlockSpec((d, tf), lambda i, j: (0, j)),
            pl.BlockSpec((d, tf), lambda i, j: (0, j)),
            pl.BlockSpec((tf, d), lambda i, j: (j, 0)),
            pl.BlockSpec((1, d), lambda i, j: (0, 0)),
        ],
        out_specs=pl.BlockSpec((tm, d), lambda i, j: (i, 0)),
        out_shape=jax.ShapeDtypeStruct((t, d), F32),
        scratch_shapes=[pltpu.VMEM((tm, d), F32)],
        compiler_params=_params(("arbitrary", "arbitrary")),
        name="ffn",
    )(h2, x1, w["w_gate"], w["w_up"], w["w_down"], w["g_post_ffn"])


def _rope_table(length):
    inv = 1.0 / (ROPE_THETA ** (jnp.arange(0, QK_ROPE, 2, dtype=F32) / QK_ROPE))
    ang = jnp.arange(length, dtype=F32)[:, None] * inv[None, :]
    cos, sin = jnp.cos(ang), jnp.sin(ang)
    return jnp.concatenate([cos, cos, -sin, sin], axis=-1)


def _swap_halves(wr):
    half = QK_ROPE // 2
    return jnp.concatenate([wr[..., half:], wr[..., :half]], axis=-1)


def _prepare_weights(g_pre_mix, w_in, g_q_a, w_q_b, g_kv_a, w_kv_b, lam_re, lam_im, log_dt,
                     b_re, b_im, c_re, c_im, ssm_d, w_glu, b_glu, g_out_attn, g_out_ssm, w_out,
                     g_post_mix, g_pre_ffn, w_gate, w_up, w_down, g_post_ffn):
    row = lambda g: g.astype(F32).reshape(1, -1)
    i1 = Q_LORA + KV_LORA
    i2 = i1 + QK_ROPE
    w_rope = w_in[:, i1:i2]
    w_in_ext = jnp.concatenate([w_in[:, :i1], w_in[:, i2:], w_rope, _swap_halves(w_rope)], axis=1)
    wq = w_q_b.reshape(Q_LORA, N_HEADS, QK_NOPE + QK_ROPE)
    wq_ext = jnp.concatenate([wq, _swap_halves(wq[..., QK_NOPE:])], axis=-1)
    wkv = w_kv_b.reshape(KV_LORA, N_HEADS, QK_NOPE + V_HEAD)
    n_a = N_HEADS * V_HEAD
    out = dict(
        n_u=w_in.shape[1] - i2,
        q_scale=float((QK_NOPE + QK_ROPE) ** -0.5 * math.log2(math.e)),
        g_pre_mix=row(g_pre_mix), w_in=w_in_ext.astype(BF16),
        g_q_a=row(g_q_a), w_q=wq_ext.reshape(Q_LORA, N_HEADS * HEAD_PAD).astype(BF16),
        g_kv_a=row(g_kv_a),
        w_k=wkv[..., :QK_NOPE].reshape(KV_LORA, N_HEADS * QK_NOPE).astype(BF16),
        w_vt=wkv[..., QK_NOPE:].reshape(KV_LORA, N_HEADS * V_HEAD).T.astype(BF16),
        ssm_d=row(ssm_d), w_glu=w_glu.astype(BF16), b_glu=row(b_glu),
        g_out_attn=row(g_out_attn), g_out_ssm=row(g_out_ssm),
        w_out_a=w_out[:n_a].astype(BF16), w_out_s=w_out[n_a:].astype(BF16),
        g_post_mix=row(g_post_mix), g_pre_ffn=row(g_pre_ffn),
        w_gate=w_gate.astype(BF16), w_up=w_up.astype(BF16), w_down=w_down.astype(BF16),
        g_post_ffn=row(g_post_ffn),
    )
    out.update(_ssm_matrices(lam_re, lam_im, log_dt, b_re, b_im, c_re, c_im))
    return out


def _layer(x, w):
    batch, seq_len, d = x.shape
    x2d = x.reshape(batch * seq_len, d)
    q, k, vt, u, uc = _in_proj(x2d, seq_len, w, _rope_table(seq_len))
    a = _attention(q, k, vt, batch, seq_len)
    ys = _ssm(uc, batch, seq_len, w)
    x1, h2 = _mix_out(ys, u, a, x2d, w)
    return _ffn(h2, x1, w).reshape(batch, seq_len, d)


def kernel(x_prompt, x_sample, g_pre_mix, w_in, g_q_a, w_q_b, g_kv_a, w_kv_b, ssm_lam_re, ssm_lam_im, ssm_log_dt, ssm_b_re, ssm_b_im, ssm_c_re, ssm_c_im, ssm_d, w_glu, b_glu, g_out_attn, g_out_ssm, w_out, g_post_mix, g_pre_ffn, w_gate, w_up, w_down, g_post_ffn):
    weights = (g_pre_mix, w_in, g_q_a, w_q_b, g_kv_a, w_kv_b, ssm_lam_re, ssm_lam_im, ssm_log_dt,
               ssm_b_re, ssm_b_im, ssm_c_re, ssm_c_im, ssm_d, w_glu, b_glu, g_out_attn, g_out_ssm,
               w_out, g_post_mix, g_pre_ffn, w_gate, w_up, w_down, g_post_ffn)
    depth = g_pre_mix.shape[0]
    for layer in range(depth):
        w = _prepare_weights(*[p[layer] for p in weights])
        x_prompt = _layer(x_prompt, w)
        x_sample = _layer(x_sample, w)
    return (x_prompt, x_sample)
```

```python
import functools
import math

import jax
import jax.numpy as jnp
import numpy as np
from jax import lax
from jax.experimental import pallas as pl
from jax.experimental.pallas import tpu as pltpu

F32 = jnp.float32
BF16 = jnp.bfloat16

RMS_EPS = 1e-6
ROPE_THETA = 10000.0
LANES = 128
SUBLANES = 8
VMEM_LIMIT_BYTES = 56 * 1024 * 1024

N_HEADS = 8
QK_NOPE = 128
QK_ROPE = 64
V_HEAD = 128
HEAD_PAD = 256
Q_LORA = 512
KV_LORA = 256
ACC_ROWS = V_HEAD + 16
ATTN_UNROLL = 4
SSM_GROUP = 16
SSM_STATE = 64
SSM_CHUNK = LANES // SSM_GROUP
SSM_PITCH_PAD = 8


def _rms(x, g):
    return x * lax.rsqrt(jnp.mean(x * x, axis=-1, keepdims=True) + RMS_EPS) * g


def _params(sem):
    return pltpu.CompilerParams(dimension_semantics=sem, vmem_limit_bytes=VMEM_LIMIT_BYTES)


def _tile(n, target):
    t = min(n, target)
    while n % t:
        t //= 2
    return t


def _block_transpose(a):
    lane = lax.broadcasted_iota(jnp.int32, a[0].shape, 1)
    for d in (4, 2, 1):
        shift = SSM_GROUP * d
        upper = (lane & shift) != 0
        new = list(a)
        for i in range(len(a)):
            if i & d:
                continue
            x, y = a[i], a[i | d]
            new[i] = jnp.where(upper, pltpu.roll(y, shift, 1), x)
            new[i | d] = jnp.where(upper, y, pltpu.roll(x, LANES - shift, 1))
        a = new
    return a


def _const_spec(shape):
    nd = len(shape)
    return pl.BlockSpec(shape, lambda *_: (0,) * nd, pipeline_mode=pl.Buffered(1))


def _in_proj_kernel(x_ref, gpre_ref, win_ref, gq_ref, wq_ref, gkv_ref, wk_ref, wvt_ref, cs_ref,
                    q_ref, k_ref, vt_ref, u_ref, uc_ref, *, q_scale):
    x = x_ref[...]
    h = _rms(x, gpre_ref[...])
    z = jnp.dot(h.astype(BF16), win_ref[...], preferred_element_type=F32)
    c_q = z[:, :Q_LORA]
    c_kv = z[:, Q_LORA:Q_LORA + KV_LORA]
    n_blk = u_ref.shape[0]
    u0 = Q_LORA + KV_LORA
    for blk in range(n_blk):
        u_ref[blk] = z[:, u0 + blk * LANES:u0 + (blk + 1) * LANES]
    k_rope = z[:, u0 + n_blk * LANES:]
    cs = cs_ref[...]

    def rotate(t):
        t = t * cs
        return t + pltpu.roll(t, QK_ROPE, 1)

    cqn = _rms(c_q, gq_ref[...]).astype(BF16)
    q = jnp.dot(cqn, wq_ref[...], preferred_element_type=F32) * q_scale
    for hd in range(N_HEADS):
        lo = hd * HEAD_PAD
        q_ref[:, lo:lo + QK_NOPE] = q[:, lo:lo + QK_NOPE].astype(BF16)
        q_ref[:, lo + QK_NOPE:lo + HEAD_PAD] = rotate(q[:, lo + QK_NOPE:lo + HEAD_PAD]).astype(BF16)

    ckvn = _rms(c_kv, gkv_ref[...]).astype(BF16)
    k_nope = jnp.dot(ckvn, wk_ref[...], preferred_element_type=F32)
    vt = lax.dot_general(wvt_ref[...], ckvn, (((1,), (1,)), ((), ())),
                         preferred_element_type=F32).astype(BF16)
    ones = jnp.ones((ACC_ROWS - V_HEAD, vt.shape[1]), BF16)
    for hd in range(N_HEADS):
        vt_ref[hd * ACC_ROWS:hd * ACC_ROWS + V_HEAD] = vt[hd * V_HEAD:(hd + 1) * V_HEAD]
        vt_ref[hd * ACC_ROWS + V_HEAD:(hd + 1) * ACC_ROWS] = ones
    lane = lax.broadcasted_iota(jnp.int32, k_rope.shape, 1)
    kr = jnp.where(lane < QK_ROPE, rotate(k_rope), 0.0).astype(BF16)
    for hd in range(N_HEADS):
        lo = hd * HEAD_PAD
        k_ref[:, lo:lo + QK_NOPE] = k_nope[:, hd * QK_NOPE:(hd + 1) * QK_NOPE].astype(BF16)
        k_ref[:, lo + QK_NOPE:lo + HEAD_PAD] = kr

    n_j = x.shape[0] // SSM_CHUNK
    for blk in range(n_blk):
        steps = [u_ref[blk, pl.ds(t, n_j, stride=SSM_CHUNK), :] for t in range(SSM_CHUNK)]
        groups = _block_transpose(steps)
        for gl in range(SUBLANES):
            uc_ref[blk * SUBLANES + gl] = groups[gl].astype(BF16)


def _in_proj(x2d, seq_len, w, cs):
    t, d = x2d.shape
    tm = _tile(seq_len, 512)
    n_blk = w["n_u"] // LANES
    n_g = w["n_u"] // SSM_GROUP
    kern = functools.partial(_in_proj_kernel, q_scale=w["q_scale"])
    tiles_per_seq = seq_len // tm
    row = lambda i: (i, 0)
    return pl.pallas_call(
        kern,
        grid=(t // tm,),
        in_specs=[
            pl.BlockSpec((tm, d), row),
            _const_spec((1, d)),
            _const_spec(w["w_in"].shape),
            _const_spec((1, Q_LORA)),
            _const_spec(w["w_q"].shape),
            _const_spec((1, KV_LORA)),
            _const_spec(w["w_k"].shape),
            _const_spec(w["w_vt"].shape),
            pl.BlockSpec((tm, LANES), lambda i: (i % tiles_per_seq, 0)),
        ],
        out_specs=[
            pl.BlockSpec((tm, N_HEADS * HEAD_PAD), row),
            pl.BlockSpec((tm, N_HEADS * HEAD_PAD), row),
            pl.BlockSpec((N_HEADS * ACC_ROWS, tm), lambda i: (0, i)),
            pl.BlockSpec((n_blk, tm, LANES), lambda i: (0, i, 0)),
            pl.BlockSpec((n_g, tm // SSM_CHUNK, LANES), lambda i: (0, i, 0)),
        ],
        out_shape=[
            jax.ShapeDtypeStruct((t, N_HEADS * HEAD_PAD), BF16),
            jax.ShapeDtypeStruct((t, N_HEADS * HEAD_PAD), BF16),
            jax.ShapeDtypeStruct((N_HEADS * ACC_ROWS, t), BF16),
            jax.ShapeDtypeStruct((n_blk, t, LANES), F32),
            jax.ShapeDtypeStruct((n_g, t // SSM_CHUNK, LANES), BF16),
        ],
        compiler_params=_params(("arbitrary",)),
        name="in_proj",
    )(x2d, w["g_pre_mix"], w["w_in"], w["g_q_a"], w["w_q"], w["g_kv_a"], w["w_k"], w["w_vt"], cs)


def _flash_kernel(q_ref, k_ref, vt_ref, o_ref, s_ref, mx_ref, m_ref, acc_ref, *, tk):
    n_k = k_ref.shape[1] // tk

    def scores(i, slot):
        start = pl.multiple_of(i * tk, tk)
        s = lax.dot_general(k_ref[0, pl.ds(start, tk), :], q_ref[0],
                            (((1,), (1,)), ((), ())), preferred_element_type=F32)
        s_ref[slot] = s
        x = s.reshape(tk // SUBLANES, SUBLANES, s.shape[-1])
        while x.shape[0] > 1:
            half = x.shape[0] // 2
            x = jnp.maximum(x[:half], x[half:])
        mx_ref[slot] = x[0]

    def update(i, slot):
        m = m_ref[...]
        m_new = jnp.maximum(m, jnp.max(mx_ref[slot], axis=0, keepdims=True))
        alpha = jnp.exp2(m - m_new)
        p = jnp.exp2(s_ref[slot] - m_new).astype(BF16)
        start = pl.multiple_of(i * tk, tk)
        pv = jnp.dot(vt_ref[:, pl.ds(start, tk)], p, preferred_element_type=F32)
        acc_ref[...] = alpha * acc_ref[...] + pv
        m_ref[...] = m_new

    m_ref[...] = jnp.full_like(m_ref, -jnp.inf)
    acc_ref[...] = jnp.zeros_like(acc_ref)
    scores(0, 0)

    unroll = ATTN_UNROLL if n_k >= 2 * ATTN_UNROLL else 2

    def body(j, _):
        for t in range(unroll):
            scores(j * unroll + t + 1, (t + 1) % 2)
            update(j * unroll + t, t % 2)
        return 0

    n_loop = (n_k - 1) // unroll
    lax.fori_loop(0, n_loop, body, 0)
    for i in range(n_loop * unroll, n_k):
        if i + 1 < n_k:
            scores(i + 1, (i + 1) % 2)
        update(i, i % 2)
    acc = acc_ref[...]
    o_ref[0] = (acc[:V_HEAD] / acc[V_HEAD:V_HEAD + 1]).T.astype(o_ref.dtype)


def _attention(q, k, vt, batch, seq_len):
    q = q.reshape(batch, seq_len, N_HEADS * HEAD_PAD)
    k = k.reshape(batch, seq_len, N_HEADS * HEAD_PAD)
    tq = _tile(seq_len, 512)
    tk = _tile(seq_len, 1024)
    out = pl.pallas_call(
        functools.partial(_flash_kernel, tk=tk),
        grid=(batch, N_HEADS, seq_len // tq),
        in_specs=[
            pl.BlockSpec((1, tq, HEAD_PAD), lambda b, h, i: (b, i, h)),
            pl.BlockSpec((1, seq_len, HEAD_PAD), lambda b, h, i: (b, 0, h)),
            pl.BlockSpec((ACC_ROWS, seq_len), lambda b, h, i: (h, b)),
        ],
        out_specs=pl.BlockSpec((1, tq, V_HEAD), lambda b, h, i: (b, i, h)),
        out_shape=jax.ShapeDtypeStruct((batch, seq_len, N_HEADS * V_HEAD), BF16),
        scratch_shapes=[pltpu.VMEM((2, tk, tq), F32), pltpu.VMEM((2, SUBLANES, tq), F32),
                        pltpu.VMEM((1, tq), F32), pltpu.VMEM((ACC_ROWS, tq), F32)],
        compiler_params=_params(("arbitrary", "arbitrary", "arbitrary")),
        name="attention",
    )(q, k, vt)
    return out.reshape(batch * seq_len, N_HEADS * V_HEAD)


def _ssm_matrices(lam_re, lam_im, log_dt, b_re, b_im, c_re, c_im):
    t_c = SSM_CHUNK
    n_g = lam_re.shape[1]
    w_in, w_state, a_r, a_s = [], [], [], []
    toep = 0.0
    for d in range(2):
        dt = jnp.exp(log_dt[d].astype(F32))[:, None]
        lam = lax.complex(lam_re[d].astype(F32), lam_im[d].astype(F32))
        a = jnp.exp(lam * dt)
        coef = (a - 1.0) / lam
        bt = lax.complex(b_re[d].astype(F32), b_im[d].astype(F32)) * coef[:, :, None]
        cc = lax.complex(c_re[d].astype(F32), c_im[d].astype(F32))
        steps = jnp.arange(t_c + 1, dtype=F32)
        apow = jnp.exp(lam[None] * dt[None] * steps[:, None, None])
        in_pow = apow[:t_c][::-1] if d == 0 else apow[:t_c]
        win = jnp.einsum("sgp,gpc->gscp", in_pow, bt, precision=lax.Precision.HIGHEST).reshape(n_g, LANES, SSM_STATE)
        w_in.append(jnp.concatenate([win.real, win.imag, win.imag, win.real], axis=-1))
        out_pow = apow[1:] if d == 0 else apow[1:][::-1]
        wst = jnp.einsum("gcp,tgp->gptc", cc, out_pow, precision=lax.Precision.HIGHEST).reshape(n_g, SSM_STATE, LANES)
        w_state.append(jnp.concatenate([wst.real, -wst.imag], axis=1))
        kern = jnp.einsum("gcp,kgp,gpe->kgce", cc, apow[:t_c], bt,
                          precision=lax.Precision.HIGHEST).real
        lag = np.arange(t_c)[None, :] - np.arange(t_c)[:, None]
        lag = lag if d == 0 else -lag
        sel = (lag[:, :, None] == np.arange(t_c)[None, None, :]).astype(np.float32)
        toep = toep + jnp.einsum("stk,kgce->gsetc", sel, kern,
                                 precision=lax.Precision.HIGHEST).reshape(n_g, LANES, LANES)
        a_t = apow[t_c]
        a_r.append(jnp.concatenate([a_t.real, a_t.real], axis=-1))
        a_s.append(jnp.concatenate([-a_t.imag, a_t.imag], axis=-1))
    w_out = jnp.concatenate([toep, w_state[0], w_state[1]], axis=1)
    return dict(w_in_f=w_in[0].astype(BF16), w_in_b=w_in[1].astype(BF16), w_out=w_out.astype(BF16),
                ar_f=a_r[0], as_f=a_s[0], ar_b=a_r[1], as_b=a_s[1])


def _ssm_state_kernel(uf_ref, ub_ref, wf_ref, wb_ref, arf_ref, asf_ref, arb_ref, asb_ref,
                      xf_ref, xb_ref, vf, vft, vb, vbt, carry, *, tj, pitch):
    n_g = uf_ref.shape[0]
    n_blk = n_g // SUBLANES

    @pl.when(pl.program_id(1) == 0)
    def _():
        carry[...] = jnp.zeros_like(carry)

    def project(blk, _):
        for gl in range(SUBLANES):
            g = blk * SUBLANES + gl
            row = pl.multiple_of(g * pitch, SUBLANES)
            pf = jnp.dot(uf_ref[g], wf_ref[g], preferred_element_type=F32)
            vf[pl.ds(row, tj), :] = pf[:, :LANES]
            vft[pl.ds(row, tj), :] = pf[:, LANES:]
            pb = jnp.dot(ub_ref[g], wb_ref[g], preferred_element_type=F32)
            vb[pl.ds(row, tj), :] = pb[:, :LANES]
            vbt[pl.ds(row, tj), :] = pb[:, LANES:]
        return 0

    lax.fori_loop(0, n_blk, project, 0)

    def scan(j, state):
        jb = tj - 1 - j
        new = []
        for blk in range(n_blk):
            base = blk * SUBLANES * pitch
            gs = pl.ds(blk * SUBLANES, SUBLANES)
            for d, (v, vt, ar_ref, as_ref, jj) in enumerate(
                    ((vf, vft, arf_ref, asf_ref, j), (vb, vbt, arb_ref, asb_ref, jb))):
                x, xt = state[(blk * 2 + d) * 2], state[(blk * 2 + d) * 2 + 1]
                rows = pl.ds(base + jj, SUBLANES, stride=pitch)
                vx = v[rows, :]
                vxt = vt[rows, :]
                v[rows, :] = x
                ar = ar_ref[gs, :]
                a_s = as_ref[gs, :]
                new.append(ar * x + a_s * xt + vx)
                new.append(ar * xt - a_s * x + vxt)
        return tuple(new)

    init = tuple(carry[i] for i in range(4 * n_blk))
    final = lax.fori_loop(0, tj, scan, init)
    for i in range(4 * n_blk):
        carry[i] = final[i]

    def emit(g, _):
        row = pl.multiple_of(g * pitch, SUBLANES)
        xf_ref[g] = vf[pl.ds(row, tj), :].astype(BF16)
        xb_ref[g] = vb[pl.ds(row, tj), :].astype(BF16)
        return 0

    lax.fori_loop(0, n_g, emit, 0, unroll=4)


def _ssm_out_kernel(u_ref, xf_ref, xb_ref, w_ref, y_ref, *, tj):
    n_blk = y_ref.shape[0]

    def body(blk, _):
        groups = []
        for gl in range(SUBLANES):
            g = blk * SUBLANES + gl
            lhs = jnp.concatenate([u_ref[g], xf_ref[g], xb_ref[g]], axis=1)
            groups.append(jnp.dot(lhs, w_ref[g], preferred_element_type=F32))
        steps = _block_transpose(groups)
        for t in range(SSM_CHUNK):
            y_ref[blk, pl.ds(t, tj, stride=SSM_CHUNK), :] = steps[t]
        return 0

    lax.fori_loop(0, n_blk, body, 0)


def _ssm(uc, batch, seq_len, w):
    n_g = uc.shape[0]
    n_j = seq_len // SSM_CHUNK
    tj = _tile(n_j, 128)
    n_t = n_j // tj
    pitch = tj + SSM_PITCH_PAD
    blk = (n_g, tj, LANES)
    fwd = lambda b, i: (0, b * n_t + i, 0)
    bwd = lambda b, i: (0, b * n_t + n_t - 1 - i, 0)
    scan_buf = pltpu.VMEM((n_g * pitch, LANES), F32)
    xf, xb = pl.pallas_call(
        functools.partial(_ssm_state_kernel, tj=tj, pitch=pitch),
        grid=(batch, n_t),
        in_specs=[
            pl.BlockSpec(blk, fwd),
            pl.BlockSpec(blk, bwd),
            _const_spec(w["w_in_f"].shape),
            _const_spec(w["w_in_b"].shape),
            _const_spec(w["ar_f"].shape),
            _const_spec(w["as_f"].shape),
            _const_spec(w["ar_b"].shape),
            _const_spec(w["as_b"].shape),
        ],
        out_specs=[pl.BlockSpec(blk, fwd), pl.BlockSpec(blk, bwd)],
        out_shape=[jax.ShapeDtypeStruct(uc.shape, BF16)] * 2,
        scratch_shapes=[scan_buf, scan_buf, scan_buf, scan_buf,
                        pltpu.VMEM((4 * (n_g // SUBLANES), SUBLANES, LANES), F32)],
        compiler_params=_params(("arbitrary", "arbitrary")),
        name="ssm_state",
    )(uc, uc, w["w_in_f"], w["w_in_b"], w["ar_f"], w["as_f"], w["ar_b"], w["as_b"])
    n_blk = n_g // SUBLANES
    return pl.pallas_call(
        functools.partial(_ssm_out_kernel, tj=tj),
        grid=(batch, n_t),
        in_specs=[pl.BlockSpec(blk, fwd), pl.BlockSpec(blk, fwd), pl.BlockSpec(blk, fwd),
                  _const_spec(w["w_out"].shape)],
        out_specs=pl.BlockSpec((n_blk, tj * SSM_CHUNK, LANES), fwd),
        out_shape=jax.ShapeDtypeStruct((n_blk, batch * seq_len, LANES), F32),
        compiler_params=_params(("arbitrary", "arbitrary")),
        name="ssm_out",
    )(uc, xf, xb, w["w_out"])


def _mix_out_kernel(ys_ref, u_ref, a_ref, x_ref, d_ref, wglu_ref, bglu_ref, ga_ref, gs_ref,
                    woa_ref, wos_ref, gpost_ref, gffn_ref, x1_ref, h2_ref):
    n_blk = ys_ref.shape[0]
    ys = jnp.concatenate([ys_ref[b] for b in range(n_blk)], axis=1)
    u = jnp.concatenate([u_ref[b] for b in range(n_blk)], axis=1)
    y = jax.nn.gelu(ys + d_ref[...] * u)
    gate = jax.nn.sigmoid(jnp.dot(y.astype(BF16), wglu_ref[...], preferred_element_type=F32)
                          + bglu_ref[...])
    s = y * gate
    an = _rms(a_ref[...].astype(F32), ga_ref[...]).astype(BF16)
    sn = _rms(s, gs_ref[...]).astype(BF16)
    m = (jnp.dot(an, woa_ref[...], preferred_element_type=F32)
         + jnp.dot(sn, wos_ref[...], preferred_element_type=F32))
    x1 = x_ref[...] + _rms(m, gpost_ref[...])
    x1_ref[...] = x1
    h2_ref[...] = _rms(x1, gffn_ref[...]).astype(BF16)


def _mix_out(ys, u, a, x2d, w):
    t, d = x2d.shape
    n_a = a.shape[1]
    n_blk = ys.shape[0]
    n_s = n_blk * LANES
    tm = _tile(t, 512)
    row = lambda i: (i, 0)
    return pl.pallas_call(
        _mix_out_kernel,
        grid=(t // tm,),
        in_specs=[
            pl.BlockSpec((n_blk, tm, LANES), lambda i: (0, i, 0)),
            pl.BlockSpec((n_blk, tm, LANES), lambda i: (0, i, 0)),
            pl.BlockSpec((tm, n_a), row),
            pl.BlockSpec((tm, d), row),
            _const_spec((1, n_s)),
            _const_spec(w["w_glu"].shape),
            _const_spec((1, n_s)),
            _const_spec((1, n_a)),
            _const_spec((1, n_s)),
            _const_spec(w["w_out_a"].shape),
            _const_spec(w["w_out_s"].shape),
            _const_spec((1, d)),
            _const_spec((1, d)),
        ],
        out_specs=[pl.BlockSpec((tm, d), row), pl.BlockSpec((tm, d), row)],
        out_shape=[jax.ShapeDtypeStruct((t, d), F32), jax.ShapeDtypeStruct((t, d), BF16)],
        compiler_params=_params(("arbitrary",)),
        name="mix_out",
    )(ys, u, a, x2d, w["ssm_d"], w["w_glu"], w["b_glu"], w["g_out_attn"], w["g_out_ssm"],
      w["w_out_a"], w["w_out_s"], w["g_post_mix"], w["g_pre_ffn"])


def _ffn_kernel(h_ref, x1_ref, wg_ref, wu_ref, wd_ref, gpost_ref, o_ref, acc_ref):
    j = pl.program_id(1)

    @pl.when(j == 0)
    def _():
        acc_ref[...] = jnp.zeros_like(acc_ref)

    h = h_ref[...]
    gate = jnp.dot(h, wg_ref[...], preferred_element_type=F32)
    up = jnp.dot(h, wu_ref[...], preferred_element_type=F32)
    act = (jax.nn.silu(gate) * up).astype(BF16)
    acc_ref[...] += jnp.dot(act, wd_ref[...], preferred_element_type=F32)

    @pl.when(j == pl.num_programs(1) - 1)
    def _():
        o_ref[...] = x1_ref[...] + _rms(acc_ref[...], gpost_ref[...])


def _ffn(h2, x1, w):
    t, d = x1.shape
    d_ff = w["w_gate"].shape[1]
    tm = _tile(t, 512)
    tf = _tile(d_ff, 512)
    return pl.pallas_call(
        _ffn_kernel,
        grid=(t // tm, d_ff // tf),
        in_specs=[
            pl.BlockSpec((tm, d), lambda i, j: (i, 0)),
            pl.BlockSpec((tm, d), lambda i, j: (i, 0)),
            pl.BlockSpec((d, tf), lambda i, j: (0, j)),
            pl.BlockSpec((d, tf), lambda i, j: (0, j)),
            pl.BlockSpec((tf, d), lambda i, j: (j, 0)),
            pl.BlockSpec((1, d), lambda i, j: (0, 0)),
        ],
        out_specs=pl.BlockSpec((tm, d), lambda i, j: (i, 0)),
        out_shape=jax.ShapeDtypeStruct((t, d), F32),
        scratch_shapes=[pltpu.VMEM((tm, d), F32)],
        compiler_params=_params(("arbitrary", "arbitrary")),
        name="ffn",
    )(h2, x1, w["w_gate"], w["w_up"], w["w_down"], w["g_post_ffn"])


def _rope_table(length):
    inv = 1.0 / (ROPE_THETA ** (jnp.arange(0, QK_ROPE, 2, dtype=F32) / QK_ROPE))
    ang = jnp.arange(length, dtype=F32)[:, None] * inv[None, :]
    cos, sin = jnp.cos(ang), jnp.sin(ang)
    return jnp.concatenate([cos, cos, -sin, sin], axis=-1)


def _swap_halves(wr):
    half = QK_ROPE // 2
    return jnp.concatenate([wr[..., half:], wr[..., :half]], axis=-1)


def _prepare_weights(g_pre_mix, w_in, g_q_a, w_q_b, g_kv_a, w_kv_b, lam_re, lam_im, log_dt,
                     b_re, b_im, c_re, c_im, ssm_d, w_glu, b_glu, g_out_attn, g_out_ssm, w_out,
                     g_post_mix, g_pre_ffn, w_gate, w_up, w_down, g_post_ffn):
    row = lambda g: g.astype(F32).reshape(1, -1)
    i1 = Q_LORA + KV_LORA
    i2 = i1 + QK_ROPE
    w_rope = w_in[:, i1:i2]
    w_in_ext = jnp.concatenate([w_in[:, :i1], w_in[:, i2:], w_rope, _swap_halves(w_rope)], axis=1)
    wq = w_q_b.reshape(Q_LORA, N_HEADS, QK_NOPE + QK_ROPE)
    wq_ext = jnp.concatenate([wq, _swap_halves(wq[..., QK_NOPE:])], axis=-1)
    wkv = w_kv_b.reshape(KV_LORA, N_HEADS, QK_NOPE + V_HEAD)
    n_a = N_HEADS * V_HEAD
    out = dict(
        n_u=w_in.shape[1] - i2,
        q_scale=float((QK_NOPE + QK_ROPE) ** -0.5 * math.log2(math.e)),
        g_pre_mix=row(g_pre_mix), w_in=w_in_ext.astype(BF16),
        g_q_a=row(g_q_a), w_q=wq_ext.reshape(Q_LORA, N_HEADS * HEAD_PAD).astype(BF16),
        g_kv_a=row(g_kv_a),
        w_k=wkv[..., :QK_NOPE].reshape(KV_LORA, N_HEADS * QK_NOPE).astype(BF16),
        w_vt=wkv[..., QK_NOPE:].reshape(KV_LORA, N_HEADS * V_HEAD).T.astype(BF16),
        ssm_d=row(ssm_d), w_glu=w_glu.astype(BF16), b_glu=row(b_glu),
        g_out_attn=row(g_out_attn), g_out_ssm=row(g_out_ssm),
        w_out_a=w_out[:n_a].astype(BF16), w_out_s=w_out[n_a:].astype(BF16),
        g_post_mix=row(g_post_mix), g_pre_ffn=row(g_pre_ffn),
        w_gate=w_gate.astype(BF16), w_up=w_up.astype(BF16), w_down=w_down.astype(BF16),
        g_post_ffn=row(g_post_ffn),
    )
    out.update(_ssm_matrices(lam_re, lam_im, log_dt, b_re, b_im, c_re, c_im))
    return out


def _layer(x, w):
    batch, seq_len, d = x.shape
    x2d = x.reshape(batch * seq_len, d)
    q, k, vt, u, uc = _in_proj(x2d, seq_len, w, _rope_table(seq_len))
    a = _attention(q, k, vt, batch, seq_len)
    ys = _ssm(uc, batch, seq_len, w)
    x1, h2 = _mix_out(ys, u, a, x2d, w)
    return _ffn(h2, x1, w).reshape(batch, seq_len, d)


def kernel(x_prompt, x_sample, g_pre_mix, w_in, g_q_a, w_q_b, g_kv_a, w_kv_b, ssm_lam_re, ssm_lam_im, ssm_log_dt, ssm_b_re, ssm_b_im, ssm_c_re, ssm_c_im, ssm_d, w_glu, b_glu, g_out_attn, g_out_ssm, w_out, g_post_mix, g_pre_ffn, w_gate, w_up, w_down, g_post_ffn):
    weights = (g_pre_mix, w_in, g_q_a, w_q_b, g_kv_a, w_kv_b, ssm_lam_re, ssm_lam_im, ssm_log_dt,
               ssm_b_re, ssm_b_im, ssm_c_re, ssm_c_im, ssm_d, w_glu, b_glu, g_out_attn, g_out_ssm,
               w_out, g_post_mix, g_pre_ffn, w_gate, w_up, w_down, g_post_ffn)
    depth = g_pre_mix.shape[0]
    for layer in range(depth):
        w = _prepare_weights(*[p[layer] for p in weights])
        x_prompt = _layer(x_prompt, w)
        x_sample = _layer(x_sample, w)
    return (x_prompt, x_sample)
```

```python
import functools
import math

import jax
import jax.numpy as jnp
import numpy as np
from jax import lax
from jax.experimental import pallas as pl
from jax.experimental.pallas import tpu as pltpu

F32 = jnp.float32
BF16 = jnp.bfloat16

RMS_EPS = 1e-6
ROPE_THETA = 10000.0
LANES = 128
SUBLANES = 8
VMEM_LIMIT_BYTES = 56 * 1024 * 1024

N_HEADS = 8
QK_NOPE = 128
QK_ROPE = 64
V_HEAD = 128
HEAD_PAD = 256
Q_LORA = 512
KV_LORA = 256
ACC_ROWS = V_HEAD + 16
ATTN_UNROLL = 4
SSM_GROUP = 16
SSM_STATE = 64
SSM_CHUNK = LANES // SSM_GROUP
SSM_PITCH_PAD = 8


def _rms(x, g):
    return x * lax.rsqrt(jnp.mean(x * x, axis=-1, keepdims=True) + RMS_EPS) * g


def _params(sem):
    return pltpu.CompilerParams(dimension_semantics=sem, vmem_limit_bytes=VMEM_LIMIT_BYTES)


def _tile(n, target):
    t = min(n, target)
    while n % t:
        t //= 2
    return t


def _block_transpose(a):
    lane = lax.broadcasted_iota(jnp.int32, a[0].shape, 1)
    for d in (4, 2, 1):
        shift = SSM_GROUP * d
        upper = (lane & shift) != 0
        new = list(a)
        for i in range(len(a)):
            if i & d:
                continue
            x, y = a[i], a[i | d]
            new[i] = jnp.where(upper, pltpu.roll(y, shift, 1), x)
            new[i | d] = jnp.where(upper, y, pltpu.roll(x, LANES - shift, 1))
        a = new
    return a


def _const_spec(shape):
    nd = len(shape)
    return pl.BlockSpec(shape, lambda *_: (0,) * nd, pipeline_mode=pl.Buffered(1))


def _in_proj_kernel(x_ref, gpre_ref, win_ref, gq_ref, wq_ref, gkv_ref, wk_ref, wvt_ref, cs_ref,
                    q_ref, k_ref, vt_ref, u_ref, uc_ref, *, q_scale):
    x = x_ref[...]
    h = _rms(x, gpre_ref[...])
    z = jnp.dot(h.astype(BF16), win_ref[...], preferred_element_type=F32)
    c_q = z[:, :Q_LORA]
    c_kv = z[:, Q_LORA:Q_LORA + KV_LORA]
    n_blk = u_ref.shape[0]
    u0 = Q_LORA + KV_LORA
    for blk in range(n_blk):
        u_ref[blk] = z[:, u0 + blk * LANES:u0 + (blk + 1) * LANES]
    k_rope = z[:, u0 + n_blk * LANES:]
    cs = cs_ref[...]

    def rotate(t):
        t = t * cs
        return t + pltpu.roll(t, QK_ROPE, 1)

    cqn = _rms(c_q, gq_ref[...]).astype(BF16)
    q = jnp.dot(cqn, wq_ref[...], preferred_element_type=F32) * q_scale
    for hd in range(N_HEADS):
        lo = hd * HEAD_PAD
        q_ref[:, lo:lo + QK_NOPE] = q[:, lo:lo + QK_NOPE].astype(BF16)
        q_ref[:, lo + QK_NOPE:lo + HEAD_PAD] = rotate(q[:, lo + QK_NOPE:lo + HEAD_PAD]).astype(BF16)

    ckvn = _rms(c_kv, gkv_ref[...]).astype(BF16)
    k_nope = jnp.dot(ckvn, wk_ref[...], preferred_element_type=F32)
    vt = lax.dot_general(wvt_ref[...], ckvn, (((1,), (1,)), ((), ())),
                         preferred_element_type=F32).astype(BF16)
    ones = jnp.ones((ACC_ROWS - V_HEAD, vt.shape[1]), BF16)
    for hd in range(N_HEADS):
        vt_ref[hd * ACC_ROWS:hd * ACC_ROWS + V_HEAD] = vt[hd * V_HEAD:(hd + 1) * V_HEAD]
        vt_ref[hd * ACC_ROWS + V_HEAD:(hd + 1) * ACC_ROWS] = ones
    lane = lax.broadcasted_iota(jnp.int32, k_rope.shape, 1)
    kr = jnp.where(lane < QK_ROPE, rotate(k_rope), 0.0).astype(BF16)
    for hd in range(N_HEADS):
        lo = hd * HEAD_PAD
        k_ref[:, lo:lo + QK_NOPE] = k_nope[:, hd * QK_NOPE:(hd + 1) * QK_NOPE].astype(BF16)
        k_ref[:, lo + QK_NOPE:lo + HEAD_PAD] = kr

    n_j = x.shape[0] // SSM_CHUNK
    for blk in range(n_blk):
        steps = [u_ref[blk, pl.ds(t, n_j, stride=SSM_CHUNK), :] for t in range(SSM_CHUNK)]
        groups = _block_transpose(steps)
        for gl in range(SUBLANES):
            uc_ref[blk * SUBLANES + gl] = groups[gl].astype(BF16)


def _in_proj(x2d, seq_len, w, cs):
    t, d = x2d.shape
    tm = _tile(seq_len, 512)
    n_blk = w["n_u"] // LANES
    n_g = w["n_u"] // SSM_GROUP
    kern = functools.partial(_in_proj_kernel, q_scale=w["q_scale"])
    tiles_per_seq = seq_len // tm
    row = lambda i: (i, 0)
    return pl.pallas_call(
        kern,
        grid=(t // tm,),
        in_specs=[
            pl.BlockSpec((tm, d), row),
            _const_spec((1, d)),
            _const_spec(w["w_in"].shape),
            _const_spec((1, Q_LORA)),
            _const_spec(w["w_q"].shape),
            _const_spec((1, KV_LORA)),
            _const_spec(w["w_k"].shape),
            _const_spec(w["w_vt"].shape),
            pl.BlockSpec((tm, LANES), lambda i: (i % tiles_per_seq, 0)),
        ],
        out_specs=[
            pl.BlockSpec((tm, N_HEADS * HEAD_PAD), row),
            pl.BlockSpec((tm, N_HEADS * HEAD_PAD), row),
            pl.BlockSpec((N_HEADS * ACC_ROWS, tm), lambda i: (0, i)),
            pl.BlockSpec((n_blk, tm, LANES), lambda i: (0, i, 0)),
            pl.BlockSpec((n_g, tm // SSM_CHUNK, LANES), lambda i: (0, i, 0)),
        ],
        out_shape=[
            jax.ShapeDtypeStruct((t, N_HEADS * HEAD_PAD), BF16),
            jax.ShapeDtypeStruct((t, N_HEADS * HEAD_PAD), BF16),
            jax.ShapeDtypeStruct((N_HEADS * ACC_ROWS, t), BF16),
            jax.ShapeDtypeStruct((n_blk, t, LANES), F32),
            jax.ShapeDtypeStruct((n_g, t // SSM_CHUNK, LANES), BF16),
        ],
        compiler_params=_params(("arbitrary",)),
        name="in_proj",
    )(x2d, w["g_pre_mix"], w["w_in"], w["g_q_a"], w["w_q"], w["g_kv_a"], w["w_k"], w["w_vt"], cs)


def _flash_kernel(q_ref, k_ref, vt_ref, o_ref, s_ref, mx_ref, m_ref, acc_ref, *, tk):
    n_k = k_ref.shape[1] // tk

    def scores(i, slot):
        start = pl.multiple_of(i * tk, tk)
        s = lax.dot_general(k_ref[0, pl.ds(start, tk), :], q_ref[0],
                            (((1,), (1,)), ((), ())), preferred_element_type=F32)
        s_ref[slot] = s
        x = s.reshape(tk // SUBLANES, SUBLANES, s.shape[-1])
        while x.shape[0] > 1:
            half = x.shape[0] // 2
            x = jnp.maximum(x[:half], x[half:])
        mx_ref[slot] = x[0]

    def update(i, slot):
        m = m_ref[...]
        m_new = jnp.maximum(m, jnp.max(mx_ref[slot], axis=0, keepdims=True))
        alpha = jnp.exp2(m - m_new)
        p = jnp.exp2(s_ref[slot] - m_new).astype(BF16)
        start = pl.multiple_of(i * tk, tk)
        pv = jnp.dot(vt_ref[:, pl.ds(start, tk)], p, preferred_element_type=F32)
        acc_ref[...] = alpha * acc_ref[...] + pv
        m_ref[...] = m_new

    m_ref[...] = jnp.full_like(m_ref, -jnp.inf)
    acc_ref[...] = jnp.zeros_like(acc_ref)
    scores(0, 0)

    unroll = ATTN_UNROLL if n_k >= 2 * ATTN_UNROLL else 2

    def body(j, _):
        for t in range(unroll):
            scores(j * unroll + t + 1, (t + 1) % 2)
            update(j * unroll + t, t % 2)
        return 0

    n_loop = (n_k - 1) // unroll
    lax.fori_loop(0, n_loop, body, 0)
    for i in range(n_loop * unroll, n_k):
        if i + 1 < n_k:
            scores(i + 1, (i + 1) % 2)
        update(i, i % 2)
    acc = acc_ref[...]
    o_ref[0] = (acc[:V_HEAD] / acc[V_HEAD:V_HEAD + 1]).T.astype(o_ref.dtype)


def _attention(q, k, vt, batch, seq_len):
    q = q.reshape(batch, seq_len, N_HEADS * HEAD_PAD)
    k = k.reshape(batch, seq_len, N_HEADS * HEAD_PAD)
    tq = _tile(seq_len, 1024)
    tk = _tile(seq_len, 1024)
    out = pl.pallas_call(
        functools.partial(_flash_kernel, tk=tk),
        grid=(batch, N_HEADS, seq_len // tq),
        in_specs=[
            pl.BlockSpec((1, tq, HEAD_PAD), lambda b, h, i: (b, i, h)),
            pl.BlockSpec((1, seq_len, HEAD_PAD), lambda b, h, i: (b, 0, h)),
            pl.BlockSpec((ACC_ROWS, seq_len), lambda b, h, i: (h, b)),
        ],
        out_specs=pl.BlockSpec((1, tq, V_HEAD), lambda b, h, i: (b, i, h)),
        out_shape=jax.ShapeDtypeStruct((batch, seq_len, N_HEADS * V_HEAD), BF16),
        scratch_shapes=[pltpu.VMEM((2, tk, tq), F32), pltpu.VMEM((2, SUBLANES, tq), F32),
                        pltpu.VMEM((1, tq), F32), pltpu.VMEM((ACC_ROWS, tq), F32)],
        compiler_params=_params(("arbitrary", "arbitrary", "arbitrary")),
        name="attention",
    )(q, k, vt)
    return out.reshape(batch * seq_len, N_HEADS * V_HEAD)


def _ssm_matrices(lam_re, lam_im, log_dt, b_re, b_im, c_re, c_im):
    t_c = SSM_CHUNK
    n_g = lam_re.shape[1]
    w_in, w_state, a_r, a_s = [], [], [], []
    toep = 0.0
    for d in range(2):
        dt = jnp.exp(log_dt[d].astype(F32))[:, None]
        lam = lax.complex(lam_re[d].astype(F32), lam_im[d].astype(F32))
        a = jnp.exp(lam * dt)
        coef = (a - 1.0) / lam
        bt = lax.complex(b_re[d].astype(F32), b_im[d].astype(F32)) * coef[:, :, None]
        cc = lax.complex(c_re[d].astype(F32), c_im[d].astype(F32))
        steps = jnp.arange(t_c + 1, dtype=F32)
        apow = jnp.exp(lam[None] * dt[None] * steps[:, None, None])
        in_pow = apow[:t_c][::-1] if d == 0 else apow[:t_c]
        win = jnp.einsum("sgp,gpc->gscp", in_pow, bt, precision=lax.Precision.HIGHEST).reshape(n_g, LANES, SSM_STATE)
        w_in.append(jnp.concatenate([win.real, win.imag, win.imag, win.real], axis=-1))
        out_pow = apow[1:] if d == 0 else apow[1:][::-1]
        wst = jnp.einsum("gcp,tgp->gptc", cc, out_pow, precision=lax.Precision.HIGHEST).reshape(n_g, SSM_STATE, LANES)
        w_state.append(jnp.concatenate([wst.real, -wst.imag], axis=1))
        kern = jnp.einsum("gcp,kgp,gpe->kgce", cc, apow[:t_c], bt,
                          precision=lax.Precision.HIGHEST).real
        lag = np.arange(t_c)[None, :] - np.arange(t_c)[:, None]
        lag = lag if d == 0 else -lag
        sel = (lag[:, :, None] == np.arange(t_c)[None, None, :]).astype(np.float32)
        toep = toep + jnp.einsum("stk,kgce->gsetc", sel, kern,
                                 precision=lax.Precision.HIGHEST).reshape(n_g, LANES, LANES)
        a_t = apow[t_c]
        a_r.append(jnp.concatenate([a_t.real, a_t.real], axis=-1))
        a_s.append(jnp.concatenate([-a_t.imag, a_t.imag], axis=-1))
    w_out = jnp.concatenate([toep, w_state[0], w_state[1]], axis=1)
    return dict(w_in_f=w_in[0].astype(BF16), w_in_b=w_in[1].astype(BF16), w_out=w_out.astype(BF16),
                ar_f=a_r[0], as_f=a_s[0], ar_b=a_r[1], as_b=a_s[1])


def _ssm_state_kernel(uf_ref, ub_ref, wf_ref, wb_ref, arf_ref, asf_ref, arb_ref, asb_ref,
                      xf_ref, xb_ref, vf, vft, vb, vbt, carry, *, tj, pitch):
    n_g = uf_ref.shape[0]
    n_blk = n_g // SUBLANES

    @pl.when(pl.program_id(1) == 0)
    def _():
        carry[...] = jnp.zeros_like(carry)

    def project(blk, _):
        for gl in range(SUBLANES):
            g = blk * SUBLANES + gl
            row = pl.multiple_of(g * pitch, SUBLANES)
            pf = jnp.dot(uf_ref[g], wf_ref[g], preferred_element_type=F32)
            vf[pl.ds(row, tj), :] = pf[:, :LANES]
            vft[pl.ds(row, tj), :] = pf[:, LANES:]
            pb = jnp.dot(ub_ref[g], wb_ref[g], preferred_element_type=F32)
            vb[pl.ds(row, tj), :] = pb[:, :LANES]
            vbt[pl.ds(row, tj), :] = pb[:, LANES:]
        return 0

    lax.fori_loop(0, n_blk, project, 0)

    def scan(j, state):
        jb = tj - 1 - j
        new = []
        for blk in range(n_blk):
            base = blk * SUBLANES * pitch
            gs = pl.ds(blk * SUBLANES, SUBLANES)
            for d, (v, vt, ar_ref, as_ref, jj) in enumerate(
                    ((vf, vft, arf_ref, asf_ref, j), (vb, vbt, arb_ref, asb_ref, jb))):
                x, xt = state[(blk * 2 + d) * 2], state[(blk * 2 + d) * 2 + 1]
                rows = pl.ds(base + jj, SUBLANES, stride=pitch)
                vx = v[rows, :]
                vxt = vt[rows, :]
                v[rows, :] = x
                ar = ar_ref[gs, :]
                a_s = as_ref[gs, :]
                new.append(ar * x + a_s * xt + vx)
                new.append(ar * xt - a_s * x + vxt)
        return tuple(new)

    init = tuple(carry[i] for i in range(4 * n_blk))
    final = lax.fori_loop(0, tj, scan, init)
    for i in range(4 * n_blk):
        carry[i] = final[i]

    def emit(g, _):
        row = pl.multiple_of(g * pitch, SUBLANES)
        xf_ref[g] = vf[pl.ds(row, tj), :].astype(BF16)
        xb_ref[g] = vb[pl.ds(row, tj), :].astype(BF16)
        return 0

    lax.fori_loop(0, n_g, emit, 0, unroll=4)


def _ssm_out_kernel(u_ref, xf_ref, xb_ref, w_ref, y_ref, *, tj):
    n_blk = y_ref.shape[0]

    def body(blk, _):
        groups = []
        for gl in range(SUBLANES):
            g = blk * SUBLANES + gl
            lhs = jnp.concatenate([u_ref[g], xf_ref[g], xb_ref[g]], axis=1)
            groups.append(jnp.dot(lhs, w_ref[g], preferred_element_type=F32))
        steps = _block_transpose(groups)
        for t in range(SSM_CHUNK):
            y_ref[blk, pl.ds(t, tj, stride=SSM_CHUNK), :] = steps[t]
        return 0

    lax.fori_loop(0, n_blk, body, 0)


def _ssm(uc, batch, seq_len, w):
    n_g = uc.shape[0]
    n_j = seq_len // SSM_CHUNK
    tj = _tile(n_j, 128)
    n_t = n_j // tj
    pitch = tj + SSM_PITCH_PAD
    blk = (n_g, tj, LANES)
    fwd = lambda b, i: (0, b * n_t + i, 0)
    bwd = lambda b, i: (0, b * n_t + n_t - 1 - i, 0)
    scan_buf = pltpu.VMEM((n_g * pitch, LANES), F32)
    xf, xb = pl.pallas_call(
        functools.partial(_ssm_state_kernel, tj=tj, pitch=pitch),
        grid=(batch, n_t),
        in_specs=[
            pl.BlockSpec(blk, fwd),
            pl.BlockSpec(blk, bwd),
            _const_spec(w["w_in_f"].shape),
            _const_spec(w["w_in_b"].shape),
            _const_spec(w["ar_f"].shape),
            _const_spec(w["as_f"].shape),
            _const_spec(w["ar_b"].shape),
            _const_spec(w["as_b"].shape),
        ],
        out_specs=[pl.BlockSpec(blk, fwd), pl.BlockSpec(blk, bwd)],
        out_shape=[jax.ShapeDtypeStruct(uc.shape, BF16)] * 2,
        scratch_shapes=[scan_buf, scan_buf, scan_buf, scan_buf,
                        pltpu.VMEM((4 * (n_g // SUBLANES), SUBLANES, LANES), F32)],
        compiler_params=_params(("arbitrary", "arbitrary")),
        name="ssm_state",
    )(uc, uc, w["w_in_f"], w["w_in_b"], w["ar_f"], w["as_f"], w["ar_b"], w["as_b"])
    n_blk = n_g // SUBLANES
    return pl.pallas_call(
        functools.partial(_ssm_out_kernel, tj=tj),
        grid=(batch, n_t),
        in_specs=[pl.BlockSpec(blk, fwd), pl.BlockSpec(blk, fwd), pl.BlockSpec(blk, fwd),
                  _const_spec(w["w_out"].shape)],
        out_specs=pl.BlockSpec((n_blk, tj * SSM_CHUNK, LANES), fwd),
        out_shape=jax.ShapeDtypeStruct((n_blk, batch * seq_len, LANES), F32),
        compiler_params=_params(("arbitrary", "arbitrary")),
        name="ssm_out",
    )(uc, xf, xb, w["w_out"])


def _mix_out_kernel(ys_ref, u_ref, a_ref, x_ref, d_ref, wglu_ref, bglu_ref, ga_ref, gs_ref,
                    woa_ref, wos_ref, gpost_ref, gffn_ref, x1_ref, h2_ref):
    n_blk = ys_ref.shape[0]
    ys = jnp.concatenate([ys_ref[b] for b in range(n_blk)], axis=1)
    u = jnp.concatenate([u_ref[b] for b in range(n_blk)], axis=1)
    y = jax.nn.gelu(ys + d_ref[...] * u)
    gate = jax.nn.sigmoid(jnp.dot(y.astype(BF16), wglu_ref[...], preferred_element_type=F32)
                          + bglu_ref[...])
    s = y * gate
    an = _rms(a_ref[...].astype(F32), ga_ref[...]).astype(BF16)
    sn = _rms(s, gs_ref[...]).astype(BF16)
    m = (jnp.dot(an, woa_ref[...], preferred_element_type=F32)
         + jnp.dot(sn, wos_ref[...], preferred_element_type=F32))
    x1 = x_ref[...] + _rms(m, gpost_ref[...])
    x1_ref[...] = x1
    h2_ref[...] = _rms(x1, gffn_ref[...]).astype(BF16)


def _mix_out(ys, u, a, x2d, w):
    t, d = x2d.shape
    n_a = a.shape[1]
    n_blk = ys.shape[0]
    n_s = n_blk * LANES
    tm = _tile(t, 512)
    row = lambda i: (i, 0)
    return pl.pallas_call(
        _mix_out_kernel,
        grid=(t // tm,),
        in_specs=[
            pl.BlockSpec((n_blk, tm, LANES), lambda i: (0, i, 0)),
            pl.BlockSpec((n_blk, tm, LANES), lambda i: (0, i, 0)),
            pl.BlockSpec((tm, n_a), row),
            pl.BlockSpec((tm, d), row),
            _const_spec((1, n_s)),
            _const_spec(w["w_glu"].shape),
            _const_spec((1, n_s)),
            _const_spec((1, n_a)),
            _const_spec((1, n_s)),
            _const_spec(w["w_out_a"].shape),
            _const_spec(w["w_out_s"].shape),
            _const_spec((1, d)),
            _const_spec((1, d)),
        ],
        out_specs=[pl.BlockSpec((tm, d), row), pl.BlockSpec((tm, d), row)],
        out_shape=[jax.ShapeDtypeStruct((t, d), F32), jax.ShapeDtypeStruct((t, d), BF16)],
        compiler_params=_params(("arbitrary",)),
        name="mix_out",
    )(ys, u, a, x2d, w["ssm_d"], w["w_glu"], w["b_glu"], w["g_out_attn"], w["g_out_ssm"],
      w["w_out_a"], w["w_out_s"], w["g_post_mix"], w["g_pre_ffn"])


def _ffn_kernel(h_ref, x1_ref, wg_ref, wu_ref, wd_ref, gpost_ref, o_ref, acc_ref):
    j = pl.program_id(1)

    @pl.when(j == 0)
    def _():
        acc_ref[...] = jnp.zeros_like(acc_ref)

    h = h_ref[...]
    gate = jnp.dot(h, wg_ref[...], preferred_element_type=F32)
    up = jnp.dot(h, wu_ref[...], preferred_element_type=F32)
    act = (jax.nn.silu(gate) * up).astype(BF16)
    acc_ref[...] += jnp.dot(act, wd_ref[...], preferred_element_type=F32)

    @pl.when(j == pl.num_programs(1) - 1)
    def _():
        o_ref[...] = x1_ref[...] + _rms(acc_ref[...], gpost_ref[...])


def _ffn(h2, x1, w):
    t, d = x1.shape
    d_ff = w["w_gate"].shape[1]
    tm = _tile(t, 512)
    tf = _tile(d_ff, 512)
    return pl.pallas_call(
        _ffn_kernel,
        grid=(t // tm, d_ff // tf),
        in_specs=[
            pl.BlockSpec((tm, d), lambda i, j: (i, 0)),
            pl.BlockSpec((tm, d), lambda i, j: (i, 0)),
            pl.BlockSpec((d, tf), lambda i, j: (0, j)),
            pl.BlockSpec((d, tf), lambda i, j: (0, j)),
            pl.BlockSpec((tf, d), lambda i, j: (j, 0)),
            pl.BlockSpec((1, d), lambda i, j: (0, 0)),
        ],
        out_specs=pl.BlockSpec((tm, d), lambda i, j: (i, 0)),
        out_shape=jax.ShapeDtypeStruct((t, d), F32),
        scratch_shapes=[pltpu.VMEM((tm, d), F32)],
        compiler_params=_params(("arbitrary", "arbitrary")),
        name="ffn",
    )(h2, x1, w["w_gate"], w["w_up"], w["w_down"], w["g_post_ffn"])


def _rope_table(length):
    assert length % LANES == 0
    inv = 1.0 / (ROPE_THETA ** (jnp.arange(0, QK_ROPE, 2, dtype=F32) / QK_ROPE))
    hi = (jnp.arange(length // LANES, dtype=F32) * LANES)[:, None] * inv[None, :]
    lo = jnp.arange(LANES, dtype=F32)[:, None] * inv[None, :]
    ch, sh = jnp.cos(hi)[:, None, :], jnp.sin(hi)[:, None, :]
    cl, sl = jnp.cos(lo)[None, :, :], jnp.sin(lo)[None, :, :]
    cos = (ch * cl - sh * sl).reshape(length, -1)
    sin = (sh * cl + ch * sl).reshape(length, -1)
    return jnp.concatenate([cos, cos, -sin, sin], axis=-1)


def _swap_halves(wr):
    half = QK_ROPE // 2
    return jnp.concatenate([wr[..., half:], wr[..., :half]], axis=-1)


def _prepare_weights(g_pre_mix, w_in, g_q_a, w_q_b, g_kv_a, w_kv_b, lam_re, lam_im, log_dt,
                     b_re, b_im, c_re, c_im, ssm_d, w_glu, b_glu, g_out_attn, g_out_ssm, w_out,
                     g_post_mix, g_pre_ffn, w_gate, w_up, w_down, g_post_ffn):
    row = lambda g: g.astype(F32).reshape(1, -1)
    i1 = Q_LORA + KV_LORA
    i2 = i1 + QK_ROPE
    w_rope = w_in[:, i1:i2]
    w_in_ext = jnp.concatenate([w_in[:, :i1], w_in[:, i2:], w_rope, _swap_halves(w_rope)], axis=1)
    wq = w_q_b.reshape(Q_LORA, N_HEADS, QK_NOPE + QK_ROPE)
    wq_ext = jnp.concatenate([wq, _swap_halves(wq[..., QK_NOPE:])], axis=-1)
    wkv = w_kv_b.reshape(KV_LORA, N_HEADS, QK_NOPE + V_HEAD)
    n_a = N_HEADS * V_HEAD
    out = dict(
        n_u=w_in.shape[1] - i2,
        q_scale=float((QK_NOPE + QK_ROPE) ** -0.5 * math.log2(math.e)),
        g_pre_mix=row(g_pre_mix), w_in=w_in_ext.astype(BF16),
        g_q_a=row(g_q_a), w_q=wq_ext.reshape(Q_LORA, N_HEADS * HEAD_PAD).astype(BF16),
        g_kv_a=row(g_kv_a),
        w_k=wkv[..., :QK_NOPE].reshape(KV_LORA, N_HEADS * QK_NOPE).astype(BF16),
        w_vt=wkv[..., QK_NOPE:].reshape(KV_LORA, N_HEADS * V_HEAD).T.astype(BF16),
        ssm_d=row(ssm_d), w_glu=w_glu.astype(BF16), b_glu=row(b_glu),
        g_out_attn=row(g_out_attn), g_out_ssm=row(g_out_ssm),
        w_out_a=w_out[:n_a].astype(BF16), w_out_s=w_out[n_a:].astype(BF16),
        g_post_mix=row(g_post_mix), g_pre_ffn=row(g_pre_ffn),
        w_gate=w_gate.astype(BF16), w_up=w_up.astype(BF16), w_down=w_down.astype(BF16),
        g_post_ffn=row(g_post_ffn),
    )
    out.update(_ssm_matrices(lam_re, lam_im, log_dt, b_re, b_im, c_re, c_im))
    return out


def _layer(x, w):
    batch, seq_len, d = x.shape
    x2d = x.reshape(batch * seq_len, d)
    q, k, vt, u, uc = _in_proj(x2d, seq_len, w, _rope_table(seq_len))
    a = _attention(q, k, vt, batch, seq_len)
    ys = _ssm(uc, batch, seq_len, w)
    x1, h2 = _mix_out(ys, u, a, x2d, w)
    return _ffn(h2, x1, w).reshape(batch, seq_len, d)


def kernel(x_prompt, x_sample, g_pre_mix, w_in, g_q_a, w_q_b, g_kv_a, w_kv_b, ssm_lam_re, ssm_lam_im, ssm_log_dt, ssm_b_re, ssm_b_im, ssm_c_re, ssm_c_im, ssm_d, w_glu, b_glu, g_out_attn, g_out_ssm, w_out, g_post_mix, g_pre_ffn, w_gate, w_up, w_down, g_post_ffn):
    weights = (g_pre_mix, w_in, g_q_a, w_q_b, g_kv_a, w_kv_b, ssm_lam_re, ssm_lam_im, ssm_log_dt,
               ssm_b_re, ssm_b_im, ssm_c_re, ssm_c_im, ssm_d, w_glu, b_glu, g_out_attn, g_out_ssm,
               w_out, g_post_mix, g_pre_ffn, w_gate, w_up, w_down, g_post_ffn)
    depth = g_pre_mix.shape[0]
    for layer in range(depth):
        w = _prepare_weights(*[p[layer] for p in weights])
        x_prompt = _layer(x_prompt, w)
        x_sample = _layer(x_sample, w)
    return (x_prompt, x_sample)
```

```python
import functools
import math

import jax
import jax.numpy as jnp
import numpy as np
from jax import lax
from jax.experimental import pallas as pl
from jax.experimental.pallas import tpu as pltpu

F32 = jnp.float32
BF16 = jnp.bfloat16

RMS_EPS = 1e-6
ROPE_THETA = 10000.0
LANES = 128
SUBLANES = 8
VMEM_LIMIT_BYTES = 56 * 1024 * 1024

N_HEADS = 8
QK_NOPE = 128
QK_ROPE = 64
V_HEAD = 128
HEAD_PAD = 256
Q_LORA = 512
KV_LORA = 256
ACC_ROWS = V_HEAD + 16
ATTN_UNROLL = 4
SSM_GROUP = 16
SSM_STATE = 64
SSM_CHUNK = LANES // SSM_GROUP
SSM_PITCH_PAD = 8


def _rms(x, g):
    return x * lax.rsqrt(jnp.mean(x * x, axis=-1, keepdims=True) + RMS_EPS) * g


def _params(sem):
    return pltpu.CompilerParams(dimension_semantics=sem, vmem_limit_bytes=VMEM_LIMIT_BYTES)


def _tile(n, target):
    t = min(n, target)
    while n % t:
        t //= 2
    return t


def _block_transpose(a):
    lane = lax.broadcasted_iota(jnp.int32, a[0].shape, 1)
    for d in (4, 2, 1):
        shift = SSM_GROUP * d
        upper = (lane & shift) != 0
        new = list(a)
        for i in range(len(a)):
            if i & d:
                continue
            x, y = a[i], a[i | d]
            new[i] = jnp.where(upper, pltpu.roll(y, shift, 1), x)
            new[i | d] = jnp.where(upper, y, pltpu.roll(x, LANES - shift, 1))
        a = new
    return a


def _const_spec(shape):
    nd = len(shape)
    return pl.BlockSpec(shape, lambda *_: (0,) * nd, pipeline_mode=pl.Buffered(1))


def _in_proj_kernel(x_ref, gpre_ref, win_ref, gq_ref, wq_ref, gkv_ref, wk_ref, wvt_ref, cs_ref,
                    q_ref, k_ref, vt_ref, u_ref, uc_ref, *, q_scale):
    x = x_ref[...]
    h = _rms(x, gpre_ref[...])
    z = jnp.dot(h.astype(BF16), win_ref[...], preferred_element_type=F32)
    c_q = z[:, :Q_LORA]
    c_kv = z[:, Q_LORA:Q_LORA + KV_LORA]
    n_blk = u_ref.shape[0]
    u0 = Q_LORA + KV_LORA
    for blk in range(n_blk):
        u_ref[blk] = z[:, u0 + blk * LANES:u0 + (blk + 1) * LANES]
    k_rope = z[:, u0 + n_blk * LANES:]
    cs = cs_ref[...]

    def rotate(t):
        t = t * cs
        return t + pltpu.roll(t, QK_ROPE, 1)

    cqn = _rms(c_q, gq_ref[...]).astype(BF16)
    q = jnp.dot(cqn, wq_ref[...], preferred_element_type=F32) * q_scale
    for hd in range(N_HEADS):
        lo = hd * HEAD_PAD
        q_ref[:, lo:lo + QK_NOPE] = q[:, lo:lo + QK_NOPE].astype(BF16)
        q_ref[:, lo + QK_NOPE:lo + HEAD_PAD] = rotate(q[:, lo + QK_NOPE:lo + HEAD_PAD]).astype(BF16)

    ckvn = _rms(c_kv, gkv_ref[...]).astype(BF16)
    k_nope = jnp.dot(ckvn, wk_ref[...], preferred_element_type=F32)
    vt = lax.dot_general(wvt_ref[...], ckvn, (((1,), (1,)), ((), ())),
                         preferred_element_type=F32).astype(BF16)
    ones = jnp.ones((ACC_ROWS - V_HEAD, vt.shape[1]), BF16)
    for hd in range(N_HEADS):
        vt_ref[hd * ACC_ROWS:hd * ACC_ROWS + V_HEAD] = vt[hd * V_HEAD:(hd + 1) * V_HEAD]
        vt_ref[hd * ACC_ROWS + V_HEAD:(hd + 1) * ACC_ROWS] = ones
    lane = lax.broadcasted_iota(jnp.int32, k_rope.shape, 1)
    kr = jnp.where(lane < QK_ROPE, rotate(k_rope), 0.0).astype(BF16)
    for hd in range(N_HEADS):
        lo = hd * HEAD_PAD
        k_ref[:, lo:lo + QK_NOPE] = k_nope[:, hd * QK_NOPE:(hd + 1) * QK_NOPE].astype(BF16)
        k_ref[:, lo + QK_NOPE:lo + HEAD_PAD] = kr

    n_j = x.shape[0] // SSM_CHUNK
    for blk in range(n_blk):
        steps = [u_ref[blk, pl.ds(t, n_j, stride=SSM_CHUNK), :] for t in range(SSM_CHUNK)]
        groups = _block_transpose(steps)
        for gl in range(SUBLANES):
            uc_ref[blk * SUBLANES + gl] = groups[gl].astype(BF16)


def _in_proj(x2d, seq_len, w, cs):
    t, d = x2d.shape
    tm = _tile(seq_len, 512)
    n_blk = w["n_u"] // LANES
    n_g = w["n_u"] // SSM_GROUP
    kern = functools.partial(_in_proj_kernel, q_scale=w["q_scale"])
    tiles_per_seq = seq_len // tm
    row = lambda i: (i, 0)
    return pl.pallas_call(
        kern,
        grid=(t // tm,),
        in_specs=[
            pl.BlockSpec((tm, d), row),
            _const_spec((1, d)),
            _const_spec(w["w_in"].shape),
            _const_spec((1, Q_LORA)),
            _const_spec(w["w_q"].shape),
            _const_spec((1, KV_LORA)),
            _const_spec(w["w_k"].shape),
            _const_spec(w["w_vt"].shape),
            pl.BlockSpec((tm, LANES), lambda i: (i % tiles_per_seq, 0)),
        ],
        out_specs=[
            pl.BlockSpec((tm, N_HEADS * HEAD_PAD), row),
            pl.BlockSpec((tm, N_HEADS * HEAD_PAD), row),
            pl.BlockSpec((N_HEADS * ACC_ROWS, tm), lambda i: (0, i)),
            pl.BlockSpec((n_blk, tm, LANES), lambda i: (0, i, 0)),
            pl.BlockSpec((n_g, tm // SSM_CHUNK, LANES), lambda i: (0, i, 0)),
        ],
        out_shape=[
            jax.ShapeDtypeStruct((t, N_HEADS * HEAD_PAD), BF16),
            jax.ShapeDtypeStruct((t, N_HEADS * HEAD_PAD), BF16),
            jax.ShapeDtypeStruct((N_HEADS * ACC_ROWS, t), BF16),
            jax.ShapeDtypeStruct((n_blk, t, LANES), F32),
            jax.ShapeDtypeStruct((n_g, t // SSM_CHUNK, LANES), BF16),
        ],
        compiler_params=_params(("arbitrary",)),
        name="in_proj",
    )(x2d, w["g_pre_mix"], w["w_in"], w["g_q_a"], w["w_q"], w["g_kv_a"], w["w_k"], w["w_vt"], cs)


def _flash_kernel(q_ref, k_ref, vt_ref, o_ref, s_ref, mx_ref, m_ref, acc_ref, *, tk):
    n_k = k_ref.shape[1] // tk

    def scores(i, slot):
        start = pl.multiple_of(i * tk, tk)
        s = lax.dot_general(k_ref[0, pl.ds(start, tk), :], q_ref[0],
                            (((1,), (1,)), ((), ())), preferred_element_type=F32)
        s_ref[slot] = s
        x = s.reshape(tk // SUBLANES, SUBLANES, s.shape[-1])
        while x.shape[0] > 1:
            half = x.shape[0] // 2
            x = jnp.maximum(x[:half], x[half:])
        mx_ref[slot] = x[0]

    def update(i, slot):
        m = m_ref[...]
        m_new = jnp.maximum(m, jnp.max(mx_ref[slot], axis=0, keepdims=True))
        alpha = jnp.exp2(m - m_new)
        p = jnp.exp2(s_ref[slot] - m_new).astype(BF16)
        start = pl.multiple_of(i * tk, tk)
        pv = jnp.dot(vt_ref[:, pl.ds(start, tk)], p, preferred_element_type=F32)
        acc_ref[...] = alpha * acc_ref[...] + pv
        m_ref[...] = m_new

    m_ref[...] = jnp.full_like(m_ref, -jnp.inf)
    acc_ref[...] = jnp.zeros_like(acc_ref)
    scores(0, 0)

    unroll = ATTN_UNROLL if n_k >= 2 * ATTN_UNROLL else 2

    def body(j, _):
        for t in range(unroll):
            scores(j * unroll + t + 1, (t + 1) % 2)
            update(j * unroll + t, t % 2)
        return 0

    n_loop = (n_k - 1) // unroll
    lax.fori_loop(0, n_loop, body, 0)
    for i in range(n_loop * unroll, n_k):
        if i + 1 < n_k:
            scores(i + 1, (i + 1) % 2)
        update(i, i % 2)
    acc = acc_ref[...]
    o_ref[0] = (acc[:V_HEAD] / acc[V_HEAD:V_HEAD + 1]).T.astype(o_ref.dtype)


def _attention(q, k, vt, batch, seq_len):
    q = q.reshape(batch, seq_len, N_HEADS * HEAD_PAD)
    k = k.reshape(batch, seq_len, N_HEADS * HEAD_PAD)
    tq = _tile(seq_len, 2048)
    tk = _tile(seq_len, 1024)
    out = pl.pallas_call(
        functools.partial(_flash_kernel, tk=tk),
        grid=(batch, N_HEADS, seq_len // tq),
        in_specs=[
            pl.BlockSpec((1, tq, HEAD_PAD), lambda b, h, i: (b, i, h)),
            pl.BlockSpec((1, seq_len, HEAD_PAD), lambda b, h, i: (b, 0, h)),
            pl.BlockSpec((ACC_ROWS, seq_len), lambda b, h, i: (h, b)),
        ],
        out_specs=pl.BlockSpec((1, tq, V_HEAD), lambda b, h, i: (b, i, h)),
        out_shape=jax.ShapeDtypeStruct((batch, seq_len, N_HEADS * V_HEAD), BF16),
        scratch_shapes=[pltpu.VMEM((2, tk, tq), F32), pltpu.VMEM((2, SUBLANES, tq), F32),
                        pltpu.VMEM((1, tq), F32), pltpu.VMEM((ACC_ROWS, tq), F32)],
        compiler_params=_params(("arbitrary", "arbitrary", "arbitrary")),
        name="attention",
    )(q, k, vt)
    return out.reshape(batch * seq_len, N_HEADS * V_HEAD)


def _ssm_matrices(lam_re, lam_im, log_dt, b_re, b_im, c_re, c_im):
    t_c = SSM_CHUNK
    n_g = lam_re.shape[1]
    w_in, w_state, a_r, a_s = [], [], [], []
    toep = 0.0
    for d in range(2):
        dt = jnp.exp(log_dt[d].astype(F32))[:, None]
        lam = lax.complex(lam_re[d].astype(F32), lam_im[d].astype(F32))
        a = jnp.exp(lam * dt)
        coef = (a - 1.0) / lam
        bt = lax.complex(b_re[d].astype(F32), b_im[d].astype(F32)) * coef[:, :, None]
        cc = lax.complex(c_re[d].astype(F32), c_im[d].astype(F32))
        steps = jnp.arange(t_c + 1, dtype=F32)
        apow = jnp.exp(lam[None] * dt[None] * steps[:, None, None])
        in_pow = apow[:t_c][::-1] if d == 0 else apow[:t_c]
        win = jnp.einsum("sgp,gpc->gscp", in_pow, bt, precision=lax.Precision.HIGHEST).reshape(n_g, LANES, SSM_STATE)
        w_in.append(jnp.concatenate([win.real, win.imag, win.imag, win.real], axis=-1))
        out_pow = apow[1:] if d == 0 else apow[1:][::-1]
        wst = jnp.einsum("gcp,tgp->gptc", cc, out_pow, precision=lax.Precision.HIGHEST).reshape(n_g, SSM_STATE, LANES)
        w_state.append(jnp.concatenate([wst.real, -wst.imag], axis=1))
        kern = jnp.einsum("gcp,kgp,gpe->kgce", cc, apow[:t_c], bt,
                          precision=lax.Precision.HIGHEST).real
        lag = np.arange(t_c)[None, :] - np.arange(t_c)[:, None]
        lag = lag if d == 0 else -lag
        sel = (lag[:, :, None] == np.arange(t_c)[None, None, :]).astype(np.float32)
        toep = toep + jnp.einsum("stk,kgce->gsetc", sel, kern,
                                 precision=lax.Precision.HIGHEST).reshape(n_g, LANES, LANES)
        a_t = apow[t_c]
        a_r.append(jnp.concatenate([a_t.real, a_t.real], axis=-1))
        a_s.append(jnp.concatenate([-a_t.imag, a_t.imag], axis=-1))
    w_out = jnp.concatenate([toep, w_state[0], w_state[1]], axis=1)
    return dict(w_in_f=w_in[0].astype(BF16), w_in_b=w_in[1].astype(BF16), w_out=w_out.astype(BF16),
                ar_f=a_r[0], as_f=a_s[0], ar_b=a_r[1], as_b=a_s[1])


def _ssm_state_kernel(uf_ref, ub_ref, wf_ref, wb_ref, arf_ref, asf_ref, arb_ref, asb_ref,
                      xf_ref, xb_ref, vf, vft, vb, vbt, carry, *, tj, pitch):
    n_g = uf_ref.shape[0]
    n_blk = n_g // SUBLANES

    @pl.when(pl.program_id(1) == 0)
    def _():
        carry[...] = jnp.zeros_like(carry)

    def project(blk, _):
        for gl in range(SUBLANES):
            g = blk * SUBLANES + gl
            row = pl.multiple_of(g * pitch, SUBLANES)
            pf = jnp.dot(uf_ref[g], wf_ref[g], preferred_element_type=F32)
            vf[pl.ds(row, tj), :] = pf[:, :LANES]
            vft[pl.ds(row, tj), :] = pf[:, LANES:]
            pb = jnp.dot(ub_ref[g], wb_ref[g], preferred_element_type=F32)
            vb[pl.ds(row, tj), :] = pb[:, :LANES]
            vbt[pl.ds(row, tj), :] = pb[:, LANES:]
        return 0

    lax.fori_loop(0, n_blk, project, 0)

    def scan(j, state):
        jb = tj - 1 - j
        new = []
        for blk in range(n_blk):
            base = blk * SUBLANES * pitch
            gs = pl.ds(blk * SUBLANES, SUBLANES)
            for d, (v, vt, ar_ref, as_ref, jj) in enumerate(
                    ((vf, vft, arf_ref, asf_ref, j), (vb, vbt, arb_ref, asb_ref, jb))):
                x, xt = state[(blk * 2 + d) * 2], state[(blk * 2 + d) * 2 + 1]
                rows = pl.ds(base + jj, SUBLANES, stride=pitch)
                vx = v[rows, :]
                vxt = vt[rows, :]
                v[rows, :] = x
                ar = ar_ref[gs, :]
                a_s = as_ref[gs, :]
                new.append(ar * x + a_s * xt + vx)
                new.append(ar * xt - a_s * x + vxt)
        return tuple(new)

    init = tuple(carry[i] for i in range(4 * n_blk))
    final = lax.fori_loop(0, tj, scan, init)
    for i in range(4 * n_blk):
        carry[i] = final[i]

    def emit(g, _):
        row = pl.multiple_of(g * pitch, SUBLANES)
        xf_ref[g] = vf[pl.ds(row, tj), :].astype(BF16)
        xb_ref[g] = vb[pl.ds(row, tj), :].astype(BF16)
        return 0

    lax.fori_loop(0, n_g, emit, 0, unroll=4)


def _ssm_out_kernel(u_ref, xf_ref, xb_ref, w_ref, y_ref, *, tj):
    n_blk = y_ref.shape[0]

    def body(blk, _):
        groups = []
        for gl in range(SUBLANES):
            g = blk * SUBLANES + gl
            lhs = jnp.concatenate([u_ref[g], xf_ref[g], xb_ref[g]], axis=1)
            groups.append(jnp.dot(lhs, w_ref[g], preferred_element_type=F32))
        steps = _block_transpose(groups)
        for t in range(SSM_CHUNK):
            y_ref[blk, pl.ds(t, tj, stride=SSM_CHUNK), :] = steps[t]
        return 0

    lax.fori_loop(0, n_blk, body, 0)


def _ssm(uc, batch, seq_len, w):
    n_g = uc.shape[0]
    n_j = seq_len // SSM_CHUNK
    tj = _tile(n_j, 128)
    n_t = n_j // tj
    pitch = tj + SSM_PITCH_PAD
    blk = (n_g, tj, LANES)
    fwd = lambda b, i: (0, b * n_t + i, 0)
    bwd = lambda b, i: (0, b * n_t + n_t - 1 - i, 0)
    scan_buf = pltpu.VMEM((n_g * pitch, LANES), F32)
    xf, xb = pl.pallas_call(
        functools.partial(_ssm_state_kernel, tj=tj, pitch=pitch),
        grid=(batch, n_t),
        in_specs=[
            pl.BlockSpec(blk, fwd),
            pl.BlockSpec(blk, bwd),
            _const_spec(w["w_in_f"].shape),
            _const_spec(w["w_in_b"].shape),
            _const_spec(w["ar_f"].shape),
            _const_spec(w["as_f"].shape),
            _const_spec(w["ar_b"].shape),
            _const_spec(w["as_b"].shape),
        ],
        out_specs=[pl.BlockSpec(blk, fwd), pl.BlockSpec(blk, bwd)],
        out_shape=[jax.ShapeDtypeStruct(uc.shape, BF16)] * 2,
        scratch_shapes=[scan_buf, scan_buf, scan_buf, scan_buf,
                        pltpu.VMEM((4 * (n_g // SUBLANES), SUBLANES, LANES), F32)],
        compiler_params=_params(("arbitrary", "arbitrary")),
        name="ssm_state",
    )(uc, uc, w["w_in_f"], w["w_in_b"], w["ar_f"], w["as_f"], w["ar_b"], w["as_b"])
    n_blk = n_g // SUBLANES
    return pl.pallas_call(
        functools.partial(_ssm_out_kernel, tj=tj),
        grid=(batch, n_t),
        in_specs=[pl.BlockSpec(blk, fwd), pl.BlockSpec(blk, fwd), pl.BlockSpec(blk, fwd),
                  _const_spec(w["w_out"].shape)],
        out_specs=pl.BlockSpec((n_blk, tj * SSM_CHUNK, LANES), fwd),
        out_shape=jax.ShapeDtypeStruct((n_blk, batch * seq_len, LANES), F32),
        compiler_params=_params(("arbitrary", "arbitrary")),
        name="ssm_out",
    )(uc, xf, xb, w["w_out"])


def _mix_out_kernel(ys_ref, u_ref, a_ref, x_ref, d_ref, wglu_ref, bglu_ref, ga_ref, gs_ref,
                    woa_ref, wos_ref, gpost_ref, gffn_ref, x1_ref, h2_ref):
    n_blk = ys_ref.shape[0]
    ys = jnp.concatenate([ys_ref[b] for b in range(n_blk)], axis=1)
    u = jnp.concatenate([u_ref[b] for b in range(n_blk)], axis=1)
    y = jax.nn.gelu(ys + d_ref[...] * u)
    gate = jax.nn.sigmoid(jnp.dot(y.astype(BF16), wglu_ref[...], preferred_element_type=F32)
                          + bglu_ref[...])
    s = y * gate
    an = _rms(a_ref[...].astype(F32), ga_ref[...]).astype(BF16)
    sn = _rms(s, gs_ref[...]).astype(BF16)
    m = (jnp.dot(an, woa_ref[...], preferred_element_type=F32)
         + jnp.dot(sn, wos_ref[...], preferred_element_type=F32))
    x1 = x_ref[...] + _rms(m, gpost_ref[...])
    x1_ref[...] = x1
    h2_ref[...] = _rms(x1, gffn_ref[...]).astype(BF16)


def _mix_out(ys, u, a, x2d, w):
    t, d = x2d.shape
    n_a = a.shape[1]
    n_blk = ys.shape[0]
    n_s = n_blk * LANES
    tm = _tile(t, 512)
    row = lambda i: (i, 0)
    return pl.pallas_call(
        _mix_out_kernel,
        grid=(t // tm,),
        in_specs=[
            pl.BlockSpec((n_blk, tm, LANES), lambda i: (0, i, 0)),
            pl.BlockSpec((n_blk, tm, LANES), lambda i: (0, i, 0)),
            pl.BlockSpec((tm, n_a), row),
            pl.BlockSpec((tm, d), row),
            _const_spec((1, n_s)),
            _const_spec(w["w_glu"].shape),
            _const_spec((1, n_s)),
            _const_spec((1, n_a)),
            _const_spec((1, n_s)),
            _const_spec(w["w_out_a"].shape),
            _const_spec(w["w_out_s"].shape),
            _const_spec((1, d)),
            _const_spec((1, d)),
        ],
        out_specs=[pl.BlockSpec((tm, d), row), pl.BlockSpec((tm, d), row)],
        out_shape=[jax.ShapeDtypeStruct((t, d), F32), jax.ShapeDtypeStruct((t, d), BF16)],
        compiler_params=_params(("arbitrary",)),
        name="mix_out",
    )(ys, u, a, x2d, w["ssm_d"], w["w_glu"], w["b_glu"], w["g_out_attn"], w["g_out_ssm"],
      w["w_out_a"], w["w_out_s"], w["g_post_mix"], w["g_pre_ffn"])


def _ffn_kernel(h_ref, x1_ref, wg_ref, wu_ref, wd_ref, gpost_ref, o_ref, acc_ref):
    j = pl.program_id(1)

    @pl.when(j == 0)
    def _():
        acc_ref[...] = jnp.zeros_like(acc_ref)

    h = h_ref[...]
    gate = jnp.dot(h, wg_ref[...], preferred_element_type=F32)
    up = jnp.dot(h, wu_ref[...], preferred_element_type=F32)
    act = (jax.nn.silu(gate) * up).astype(BF16)
    acc_ref[...] += jnp.dot(act, wd_ref[...], preferred_element_type=F32)

    @pl.when(j == pl.num_programs(1) - 1)
    def _():
        o_ref[...] = x1_ref[...] + _rms(acc_ref[...], gpost_ref[...])


def _ffn(h2, x1, w):
    t, d = x1.shape
    d_ff = w["w_gate"].shape[1]
    tm = _tile(t, 512)
    tf = _tile(d_ff, 512)
    return pl.pallas_call(
        _ffn_kernel,
        grid=(t // tm, d_ff // tf),
        in_specs=[
            pl.BlockSpec((tm, d), lambda i, j: (i, 0)),
            pl.BlockSpec((tm, d), lambda i, j: (i, 0)),
            pl.BlockSpec((d, tf), lambda i, j: (0, j)),
            pl.BlockSpec((d, tf), lambda i, j: (0, j)),
            pl.BlockSpec((tf, d), lambda i, j: (j, 0)),
            pl.BlockSpec((1, d), lambda i, j: (0, 0)),
        ],
        out_specs=pl.BlockSpec((tm, d), lambda i, j: (i, 0)),
        out_shape=jax.ShapeDtypeStruct((t, d), F32),
        scratch_shapes=[pltpu.VMEM((tm, d), F32)],
        compiler_params=_params(("arbitrary", "arbitrary")),
        name="ffn",
    )(h2, x1, w["w_gate"], w["w_up"], w["w_down"], w["g_post_ffn"])


def _rope_table(length):
    assert length % LANES == 0
    inv = 1.0 / (ROPE_THETA ** (jnp.arange(0, QK_ROPE, 2, dtype=F32) / QK_ROPE))
    hi = (jnp.arange(length // LANES, dtype=F32) * LANES)[:, None] * inv[None, :]
    lo = jnp.arange(LANES, dtype=F32)[:, None] * inv[None, :]
    ch, sh = jnp.cos(hi)[:, None, :], jnp.sin(hi)[:, None, :]
    cl, sl = jnp.cos(lo)[None, :, :], jnp.sin(lo)[None, :, :]
    cos = (ch * cl - sh * sl).reshape(length, -1)
    sin = (sh * cl + ch * sl).reshape(length, -1)
    return jnp.concatenate([cos, cos, -sin, sin], axis=-1)


def _swap_halves(wr):
    half = QK_ROPE // 2
    return jnp.concatenate([wr[..., half:], wr[..., :half]], axis=-1)


def _prepare_weights(g_pre_mix, w_in, g_q_a, w_q_b, g_kv_a, w_kv_b, lam_re, lam_im, log_dt,
                     b_re, b_im, c_re, c_im, ssm_d, w_glu, b_glu, g_out_attn, g_out_ssm, w_out,
                     g_post_mix, g_pre_ffn, w_gate, w_up, w_down, g_post_ffn):
    row = lambda g: g.astype(F32).reshape(1, -1)
    i1 = Q_LORA + KV_LORA
    i2 = i1 + QK_ROPE
    w_rope = w_in[:, i1:i2]
    w_in_ext = jnp.concatenate([w_in[:, :i1], w_in[:, i2:], w_rope, _swap_halves(w_rope)], axis=1)
    wq = w_q_b.reshape(Q_LORA, N_HEADS, QK_NOPE + QK_ROPE)
    wq_ext = jnp.concatenate([wq, _swap_halves(wq[..., QK_NOPE:])], axis=-1)
    wkv = w_kv_b.reshape(KV_LORA, N_HEADS, QK_NOPE + V_HEAD)
    n_a = N_HEADS * V_HEAD
    out = dict(
        n_u=w_in.shape[1] - i2,
        q_scale=float((QK_NOPE + QK_ROPE) ** -0.5 * math.log2(math.e)),
        g_pre_mix=row(g_pre_mix), w_in=w_in_ext.astype(BF16),
        g_q_a=row(g_q_a), w_q=wq_ext.reshape(Q_LORA, N_HEADS * HEAD_PAD).astype(BF16),
        g_kv_a=row(g_kv_a),
        w_k=wkv[..., :QK_NOPE].reshape(KV_LORA, N_HEADS * QK_NOPE).astype(BF16),
        w_vt=wkv[..., QK_NOPE:].reshape(KV_LORA, N_HEADS * V_HEAD).T.astype(BF16),
        ssm_d=row(ssm_d), w_glu=w_glu.astype(BF16), b_glu=row(b_glu),
        g_out_attn=row(g_out_attn), g_out_ssm=row(g_out_ssm),
        w_out_a=w_out[:n_a].astype(BF16), w_out_s=w_out[n_a:].astype(BF16),
        g_post_mix=row(g_post_mix), g_pre_ffn=row(g_pre_ffn),
        w_gate=w_gate.astype(BF16), w_up=w_up.astype(BF16), w_down=w_down.astype(BF16),
        g_post_ffn=row(g_post_ffn),
    )
    out.update(_ssm_matrices(lam_re, lam_im, log_dt, b_re, b_im, c_re, c_im))
    return out


def _layer(x, w):
    batch, seq_len, d = x.shape
    x2d = x.reshape(batch * seq_len, d)
    q, k, vt, u, uc = _in_proj(x2d, seq_len, w, _rope_table(seq_len))
    a = _attention(q, k, vt, batch, seq_len)
    ys = _ssm(uc, batch, seq_len, w)
    x1, h2 = _mix_out(ys, u, a, x2d, w)
    return _ffn(h2, x1, w).reshape(batch, seq_len, d)


def kernel(x_prompt, x_sample, g_pre_mix, w_in, g_q_a, w_q_b, g_kv_a, w_kv_b, ssm_lam_re, ssm_lam_im, ssm_log_dt, ssm_b_re, ssm_b_im, ssm_c_re, ssm_c_im, ssm_d, w_glu, b_glu, g_out_attn, g_out_ssm, w_out, g_post_mix, g_pre_ffn, w_gate, w_up, w_down, g_post_ffn):
    weights = (g_pre_mix, w_in, g_q_a, w_q_b, g_kv_a, w_kv_b, ssm_lam_re, ssm_lam_im, ssm_log_dt,
               ssm_b_re, ssm_b_im, ssm_c_re, ssm_c_im, ssm_d, w_glu, b_glu, g_out_attn, g_out_ssm,
               w_out, g_post_mix, g_pre_ffn, w_gate, w_up, w_down, g_post_ffn)
    depth = g_pre_mix.shape[0]
    for layer in range(depth):
        w = _prepare_weights(*[p[layer] for p in weights])
        x_prompt = _layer(x_prompt, w)
        x_sample = _layer(x_sample, w)
    return (x_prompt, x_sample)
```

```python
import functools
import math

import jax
import jax.numpy as jnp
import numpy as np
from jax import lax
from jax.experimental import pallas as pl
from jax.experimental.pallas import tpu as pltpu

F32 = jnp.float32
BF16 = jnp.bfloat16

RMS_EPS = 1e-6
ROPE_THETA = 10000.0
LANES = 128
SUBLANES = 8
VMEM_LIMIT_BYTES = 56 * 1024 * 1024

N_HEADS = 8
QK_NOPE = 128
QK_ROPE = 64
V_HEAD = 128
HEAD_PAD = 256
Q_LORA = 512
KV_LORA = 256
ACC_ROWS = V_HEAD + 16
ATTN_UNROLL = 4
SSM_GROUP = 16
SSM_STATE = 64
SSM_CHUNK = LANES // SSM_GROUP
SSM_PITCH_PAD = 8


def _rms(x, g):
    return x * lax.rsqrt(jnp.mean(x * x, axis=-1, keepdims=True) + RMS_EPS) * g


def _params(sem):
    return pltpu.CompilerParams(dimension_semantics=sem, vmem_limit_bytes=VMEM_LIMIT_BYTES)


def _tile(n, target):
    t = min(n, target)
    while n % t:
        t //= 2
    return t


def _block_transpose(a):
    lane = lax.broadcasted_iota(jnp.int32, a[0].shape, 1)
    for d in (4, 2, 1):
        shift = SSM_GROUP * d
        upper = (lane & shift) != 0
        new = list(a)
        for i in range(len(a)):
            if i & d:
                continue
            x, y = a[i], a[i | d]
            new[i] = jnp.where(upper, pltpu.roll(y, shift, 1), x)
            new[i | d] = jnp.where(upper, y, pltpu.roll(x, LANES - shift, 1))
        a = new
    return a


def _const_spec(shape):
    nd = len(shape)
    return pl.BlockSpec(shape, lambda *_: (0,) * nd, pipeline_mode=pl.Buffered(1))


def _in_proj_kernel(x_ref, gpre_ref, win_ref, gq_ref, wqt_ref, gkv_ref, wk_ref, wvt_ref, cs_ref, cst_ref,
                    qt_ref, k_ref, vt_ref, u_ref, uc_ref, *, q_scale):
    x = x_ref[...]
    h = _rms(x, gpre_ref[...])
    z = jnp.dot(h.astype(BF16), win_ref[...], preferred_element_type=F32)
    c_q = z[:, :Q_LORA]
    c_kv = z[:, Q_LORA:Q_LORA + KV_LORA]
    n_blk = u_ref.shape[0]
    u0 = Q_LORA + KV_LORA
    for blk in range(n_blk):
        u_ref[blk] = z[:, u0 + blk * LANES:u0 + (blk + 1) * LANES]
    k_rope = z[:, u0 + n_blk * LANES:]
    cs = cs_ref[...]

    def rotate(t):
        t = t * cs
        return t + pltpu.roll(t, QK_ROPE, 1)

    cqn = _rms(c_q, gq_ref[...]).astype(BF16)
    qt = lax.dot_general(wqt_ref[...], cqn, (((1,), (1,)), ((), ())),
                         preferred_element_type=F32) * q_scale
    cst = cst_ref[...]
    for hd in range(N_HEADS):
        lo = hd * HEAD_PAD
        qt_ref[lo:lo + QK_NOPE] = qt[lo:lo + QK_NOPE].astype(BF16)
        t = qt[lo + QK_NOPE:lo + HEAD_PAD] * cst
        roped = (t[:QK_ROPE] + t[QK_ROPE:]).astype(BF16)
        qt_ref[lo + QK_NOPE:lo + QK_NOPE + QK_ROPE] = roped
        qt_ref[lo + QK_NOPE + QK_ROPE:lo + HEAD_PAD] = roped

    ckvn = _rms(c_kv, gkv_ref[...]).astype(BF16)
    k_nope = jnp.dot(ckvn, wk_ref[...], preferred_element_type=F32)
    vt = lax.dot_general(wvt_ref[...], ckvn, (((1,), (1,)), ((), ())),
                         preferred_element_type=F32).astype(BF16)
    ones = jnp.ones((ACC_ROWS - V_HEAD, vt.shape[1]), BF16)
    for hd in range(N_HEADS):
        vt_ref[hd * ACC_ROWS:hd * ACC_ROWS + V_HEAD] = vt[hd * V_HEAD:(hd + 1) * V_HEAD]
        vt_ref[hd * ACC_ROWS + V_HEAD:(hd + 1) * ACC_ROWS] = ones
    lane = lax.broadcasted_iota(jnp.int32, k_rope.shape, 1)
    kr = jnp.where(lane < QK_ROPE, rotate(k_rope), 0.0).astype(BF16)
    for hd in range(N_HEADS):
        lo = hd * HEAD_PAD
        k_ref[:, lo:lo + QK_NOPE] = k_nope[:, hd * QK_NOPE:(hd + 1) * QK_NOPE].astype(BF16)
        k_ref[:, lo + QK_NOPE:lo + HEAD_PAD] = kr

    n_j = x.shape[0] // SSM_CHUNK
    for blk in range(n_blk):
        steps = [u_ref[blk, pl.ds(t, n_j, stride=SSM_CHUNK), :] for t in range(SSM_CHUNK)]
        groups = _block_transpose(steps)
        for gl in range(SUBLANES):
            uc_ref[blk * SUBLANES + gl] = groups[gl].astype(BF16)


def _in_proj(x2d, seq_len, w, cs, cst):
    t, d = x2d.shape
    tm = _tile(seq_len, 512)
    n_blk = w["n_u"] // LANES
    n_g = w["n_u"] // SSM_GROUP
    kern = functools.partial(_in_proj_kernel, q_scale=w["q_scale"])
    tiles_per_seq = seq_len // tm
    row = lambda i: (i, 0)
    return pl.pallas_call(
        kern,
        grid=(t // tm,),
        in_specs=[
            pl.BlockSpec((tm, d), row),
            _const_spec((1, d)),
            _const_spec(w["w_in"].shape),
            _const_spec((1, Q_LORA)),
            _const_spec(w["w_qt"].shape),
            _const_spec((1, KV_LORA)),
            _const_spec(w["w_k"].shape),
            _const_spec(w["w_vt"].shape),
            pl.BlockSpec((tm, LANES), lambda i: (i % tiles_per_seq, 0)),
            pl.BlockSpec((LANES, tm), lambda i: (0, i % tiles_per_seq)),
        ],
        out_specs=[
            pl.BlockSpec((N_HEADS * HEAD_PAD, tm), lambda i: (0, i)),
            pl.BlockSpec((tm, N_HEADS * HEAD_PAD), row),
            pl.BlockSpec((N_HEADS * ACC_ROWS, tm), lambda i: (0, i)),
            pl.BlockSpec((n_blk, tm, LANES), lambda i: (0, i, 0)),
            pl.BlockSpec((n_g, tm // SSM_CHUNK, LANES), lambda i: (0, i, 0)),
        ],
        out_shape=[
            jax.ShapeDtypeStruct((N_HEADS * HEAD_PAD, t), BF16),
            jax.ShapeDtypeStruct((t, N_HEADS * HEAD_PAD), BF16),
            jax.ShapeDtypeStruct((N_HEADS * ACC_ROWS, t), BF16),
            jax.ShapeDtypeStruct((n_blk, t, LANES), F32),
            jax.ShapeDtypeStruct((n_g, t // SSM_CHUNK, LANES), BF16),
        ],
        compiler_params=_params(("arbitrary",)),
        name="in_proj",
    )(x2d, w["g_pre_mix"], w["w_in"], w["g_q_a"], w["w_qt"], w["g_kv_a"], w["w_k"], w["w_vt"], cs, cst)


def _flash_kernel(qt_ref, k_ref, vt_ref, o_ref, s_ref, mx_ref, m_ref, acc_ref, *, tk):
    n_k = k_ref.shape[1] // tk

    def scores(i, slot):
        start = pl.multiple_of(i * tk, tk)
        s = jnp.dot(k_ref[0, pl.ds(start, tk), :], qt_ref[...],
                    preferred_element_type=F32)
        s_ref[slot] = s
        x = s.reshape(tk // SUBLANES, SUBLANES, s.shape[-1])
        while x.shape[0] > 1:
            half = x.shape[0] // 2
            x = jnp.maximum(x[:half], x[half:])
        mx_ref[slot] = x[0]

    def update(i, slot):
        m = m_ref[...]
        m_new = jnp.maximum(m, jnp.max(mx_ref[slot], axis=0, keepdims=True))
        alpha = jnp.exp2(m - m_new)
        p = jnp.exp2(s_ref[slot] - m_new).astype(BF16)
        start = pl.multiple_of(i * tk, tk)
        pv = jnp.dot(vt_ref[:, pl.ds(start, tk)], p, preferred_element_type=F32)
        acc_ref[...] = alpha * acc_ref[...] + pv
        m_ref[...] = m_new

    m_ref[...] = jnp.full_like(m_ref, -jnp.inf)
    acc_ref[...] = jnp.zeros_like(acc_ref)
    scores(0, 0)

    unroll = ATTN_UNROLL if n_k >= 2 * ATTN_UNROLL else 2

    def body(j, _):
        for t in range(unroll):
            scores(j * unroll + t + 1, (t + 1) % 2)
            update(j * unroll + t, t % 2)
        return 0

    n_loop = (n_k - 1) // unroll
    lax.fori_loop(0, n_loop, body, 0)
    for i in range(n_loop * unroll, n_k):
        if i + 1 < n_k:
            scores(i + 1, (i + 1) % 2)
        update(i, i % 2)
    acc = acc_ref[...]
    o_ref[0] = (acc[:V_HEAD] / acc[V_HEAD:V_HEAD + 1]).T.astype(o_ref.dtype)


def _attention(qt, k, vt, batch, seq_len):
    k = k.reshape(batch, seq_len, N_HEADS * HEAD_PAD)
    tq = _tile(seq_len, 2048)
    tk = _tile(seq_len, 1024)
    out = pl.pallas_call(
        functools.partial(_flash_kernel, tk=tk),
        grid=(batch, N_HEADS, seq_len // tq),
        in_specs=[
            pl.BlockSpec((HEAD_PAD, tq), lambda b, h, i: (h, b * (seq_len // tq) + i)),
            pl.BlockSpec((1, seq_len, HEAD_PAD), lambda b, h, i: (b, 0, h)),
            pl.BlockSpec((ACC_ROWS, seq_len), lambda b, h, i: (h, b)),
        ],
        out_specs=pl.BlockSpec((1, tq, V_HEAD), lambda b, h, i: (b, i, h)),
        out_shape=jax.ShapeDtypeStruct((batch, seq_len, N_HEADS * V_HEAD), BF16),
        scratch_shapes=[pltpu.VMEM((2, tk, tq), F32), pltpu.VMEM((2, SUBLANES, tq), F32),
                        pltpu.VMEM((1, tq), F32), pltpu.VMEM((ACC_ROWS, tq), F32)],
        compiler_params=_params(("arbitrary", "arbitrary", "arbitrary")),
        name="attention",
    )(qt, k, vt)
    return out.reshape(batch * seq_len, N_HEADS * V_HEAD)


def _ssm_matrices(lam_re, lam_im, log_dt, b_re, b_im, c_re, c_im):
    t_c = SSM_CHUNK
    n_g = lam_re.shape[1]
    w_in, w_state, a_r, a_s = [], [], [], []
    toep = 0.0
    for d in range(2):
        dt = jnp.exp(log_dt[d].astype(F32))[:, None]
        lam = lax.complex(lam_re[d].astype(F32), lam_im[d].astype(F32))
        a = jnp.exp(lam * dt)
        coef = (a - 1.0) / lam
        bt = lax.complex(b_re[d].astype(F32), b_im[d].astype(F32)) * coef[:, :, None]
        cc = lax.complex(c_re[d].astype(F32), c_im[d].astype(F32))
        steps = jnp.arange(t_c + 1, dtype=F32)
        apow = jnp.exp(lam[None] * dt[None] * steps[:, None, None])
        in_pow = apow[:t_c][::-1] if d == 0 else apow[:t_c]
        win = jnp.einsum("sgp,gpc->gscp", in_pow, bt, precision=lax.Precision.HIGHEST).reshape(n_g, LANES, SSM_STATE)
        w_in.append(jnp.concatenate([win.real, win.imag, win.imag, win.real], axis=-1))
        out_pow = apow[1:] if d == 0 else apow[1:][::-1]
        wst = jnp.einsum("gcp,tgp->gptc", cc, out_pow, precision=lax.Precision.HIGHEST).reshape(n_g, SSM_STATE, LANES)
        w_state.append(jnp.concatenate([wst.real, -wst.imag], axis=1))
        kern = jnp.einsum("gcp,kgp,gpe->kgce", cc, apow[:t_c], bt,
                          precision=lax.Precision.HIGHEST).real
        lag = np.arange(t_c)[None, :] - np.arange(t_c)[:, None]
        lag = lag if d == 0 else -lag
        sel = (lag[:, :, None] == np.arange(t_c)[None, None, :]).astype(np.float32)
        toep = toep + jnp.einsum("stk,kgce->gsetc", sel, kern,
                                 precision=lax.Precision.HIGHEST).reshape(n_g, LANES, LANES)
        a_t = apow[t_c]
        a_r.append(jnp.concatenate([a_t.real, a_t.real], axis=-1))
        a_s.append(jnp.concatenate([-a_t.imag, a_t.imag], axis=-1))
    w_out = jnp.concatenate([toep, w_state[0], w_state[1]], axis=1)
    return dict(w_in_f=w_in[0].astype(BF16), w_in_b=w_in[1].astype(BF16), w_out=w_out.astype(BF16),
                ar_f=a_r[0], as_f=a_s[0], ar_b=a_r[1], as_b=a_s[1])


def _ssm_state_kernel(uf_ref, ub_ref, wf_ref, wb_ref, arf_ref, asf_ref, arb_ref, asb_ref,
                      xf_ref, xb_ref, vf, vft, vb, vbt, carry, *, tj, pitch):
    n_g = uf_ref.shape[0]
    n_blk = n_g // SUBLANES

    @pl.when(pl.program_id(1) == 0)
    def _():
        carry[...] = jnp.zeros_like(carry)

    def project(blk, _):
        for gl in range(SUBLANES):
            g = blk * SUBLANES + gl
            row = pl.multiple_of(g * pitch, SUBLANES)
            pf = jnp.dot(uf_ref[g], wf_ref[g], preferred_element_type=F32)
            vf[pl.ds(row, tj), :] = pf[:, :LANES]
            vft[pl.ds(row, tj), :] = pf[:, LANES:]
            pb = jnp.dot(ub_ref[g], wb_ref[g], preferred_element_type=F32)
            vb[pl.ds(row, tj), :] = pb[:, :LANES]
            vbt[pl.ds(row, tj), :] = pb[:, LANES:]
        return 0

    lax.fori_loop(0, n_blk, project, 0)

    def scan(j, state):
        jb = tj - 1 - j
        new = []
        for blk in range(n_blk):
            base = blk * SUBLANES * pitch
            gs = pl.ds(blk * SUBLANES, SUBLANES)
            for d, (v, vt, ar_ref, as_ref, jj) in enumerate(
                    ((vf, vft, arf_ref, asf_ref, j), (vb, vbt, arb_ref, asb_ref, jb))):
                x, xt = state[(blk * 2 + d) * 2], state[(blk * 2 + d) * 2 + 1]
                rows = pl.ds(base + jj, SUBLANES, stride=pitch)
                vx = v[rows, :]
                vxt = vt[rows, :]
                v[rows, :] = x
                ar = ar_ref[gs, :]
                a_s = as_ref[gs, :]
                new.append(ar * x + a_s * xt + vx)
                new.append(ar * xt - a_s * x + vxt)
        return tuple(new)

    init = tuple(carry[i] for i in range(4 * n_blk))
    final = lax.fori_loop(0, tj, scan, init)
    for i in range(4 * n_blk):
        carry[i] = final[i]

    def emit(g, _):
        row = pl.multiple_of(g * pitch, SUBLANES)
        xf_ref[g] = vf[pl.ds(row, tj), :].astype(BF16)
        xb_ref[g] = vb[pl.ds(row, tj), :].astype(BF16)
        return 0

    lax.fori_loop(0, n_g, emit, 0, unroll=4)


def _ssm_out_kernel(u_ref, xf_ref, xb_ref, w_ref, y_ref, *, tj):
    n_blk = y_ref.shape[0]

    def body(blk, _):
        groups = []
        for gl in range(SUBLANES):
            g = blk * SUBLANES + gl
            lhs = jnp.concatenate([u_ref[g], xf_ref[g], xb_ref[g]], axis=1)
            groups.append(jnp.dot(lhs, w_ref[g], preferred_element_type=F32))
        steps = _block_transpose(groups)
        for t in range(SSM_CHUNK):
            y_ref[blk, pl.ds(t, tj, stride=SSM_CHUNK), :] = steps[t]
        return 0

    lax.fori_loop(0, n_blk, body, 0)


def _ssm(uc, batch, seq_len, w):
    n_g = uc.shape[0]
    n_j = seq_len // SSM_CHUNK
    tj = _tile(n_j, 128)
    n_t = n_j // tj
    pitch = tj + SSM_PITCH_PAD
    blk = (n_g, tj, LANES)
    fwd = lambda b, i: (0, b * n_t + i, 0)
    bwd = lambda b, i: (0, b * n_t + n_t - 1 - i, 0)
    scan_buf = pltpu.VMEM((n_g * pitch, LANES), F32)
    xf, xb = pl.pallas_call(
        functools.partial(_ssm_state_kernel, tj=tj, pitch=pitch),
        grid=(batch, n_t),
        in_specs=[
            pl.BlockSpec(blk, fwd),
            pl.BlockSpec(blk, bwd),
            _const_spec(w["w_in_f"].shape),
            _const_spec(w["w_in_b"].shape),
            _const_spec(w["ar_f"].shape),
            _const_spec(w["as_f"].shape),
            _const_spec(w["ar_b"].shape),
            _const_spec(w["as_b"].shape),
        ],
        out_specs=[pl.BlockSpec(blk, fwd), pl.BlockSpec(blk, bwd)],
        out_shape=[jax.ShapeDtypeStruct(uc.shape, BF16)] * 2,
        scratch_shapes=[scan_buf, scan_buf, scan_buf, scan_buf,
                        pltpu.VMEM((4 * (n_g // SUBLANES), SUBLANES, LANES), F32)],
        compiler_params=_params(("arbitrary", "arbitrary")),
        name="ssm_state",
    )(uc, uc, w["w_in_f"], w["w_in_b"], w["ar_f"], w["as_f"], w["ar_b"], w["as_b"])
    n_blk = n_g // SUBLANES
    return pl.pallas_call(
        functools.partial(_ssm_out_kernel, tj=tj),
        grid=(batch, n_t),
        in_specs=[pl.BlockSpec(blk, fwd), pl.BlockSpec(blk, fwd), pl.BlockSpec(blk, fwd),
                  _const_spec(w["w_out"].shape)],
        out_specs=pl.BlockSpec((n_blk, tj * SSM_CHUNK, LANES), fwd),
        out_shape=jax.ShapeDtypeStruct((n_blk, batch * seq_len, LANES), F32),
        compiler_params=_params(("arbitrary", "arbitrary")),
        name="ssm_out",
    )(uc, xf, xb, w["w_out"])


def _mix_out_kernel(ys_ref, u_ref, a_ref, x_ref, d_ref, wglu_ref, bglu_ref, ga_ref, gs_ref,
                    woa_ref, wos_ref, gpost_ref, gffn_ref, x1_ref, h2_ref):
    n_blk = ys_ref.shape[0]
    ys = jnp.concatenate([ys_ref[b] for b in range(n_blk)], axis=1)
    u = jnp.concatenate([u_ref[b] for b in range(n_blk)], axis=1)
    y = jax.nn.gelu(ys + d_ref[...] * u)
    gate = jax.nn.sigmoid(jnp.dot(y.astype(BF16), wglu_ref[...], preferred_element_type=F32)
                          + bglu_ref[...])
    s = y * gate
    an = _rms(a_ref[...].astype(F32), ga_ref[...]).astype(BF16)
    sn = _rms(s, gs_ref[...]).astype(BF16)
    m = (jnp.dot(an, woa_ref[...], preferred_element_type=F32)
         + jnp.dot(sn, wos_ref[...], preferred_element_type=F32))
    x1 = x_ref[...] + _rms(m, gpost_ref[...])
    x1_ref[...] = x1
    h2_ref[...] = _rms(x1, gffn_ref[...]).astype(BF16)


def _mix_out(ys, u, a, x2d, w):
    t, d = x2d.shape
    n_a = a.shape[1]
    n_blk = ys.shape[0]
    n_s = n_blk * LANES
    tm = _tile(t, 512)
    row = lambda i: (i, 0)
    return pl.pallas_call(
        _mix_out_kernel,
        grid=(t // tm,),
        in_specs=[
            pl.BlockSpec((n_blk, tm, LANES), lambda i: (0, i, 0)),
            pl.BlockSpec((n_blk, tm, LANES), lambda i: (0, i, 0)),
            pl.BlockSpec((tm, n_a), row),
            pl.BlockSpec((tm, d), row),
            _const_spec((1, n_s)),
            _const_spec(w["w_glu"].shape),
            _const_spec((1, n_s)),
            _const_spec((1, n_a)),
            _const_spec((1, n_s)),
            _const_spec(w["w_out_a"].shape),
            _const_spec(w["w_out_s"].shape),
            _const_spec((1, d)),
            _const_spec((1, d)),
        ],
        out_specs=[pl.BlockSpec((tm, d), row), pl.BlockSpec((tm, d), row)],
        out_shape=[jax.ShapeDtypeStruct((t, d), F32), jax.ShapeDtypeStruct((t, d), BF16)],
        compiler_params=_params(("arbitrary",)),
        name="mix_out",
    )(ys, u, a, x2d, w["ssm_d"], w["w_glu"], w["b_glu"], w["g_out_attn"], w["g_out_ssm"],
      w["w_out_a"], w["w_out_s"], w["g_post_mix"], w["g_pre_ffn"])


def _ffn_kernel(h_ref, x1_ref, wg_ref, wu_ref, wd_ref, gpost_ref, o_ref, acc_ref):
    j = pl.program_id(1)

    @pl.when(j == 0)
    def _():
        acc_ref[...] = jnp.zeros_like(acc_ref)

    h = h_ref[...]
    gate = jnp.dot(h, wg_ref[...], preferred_element_type=F32)
    up = jnp.dot(h, wu_ref[...], preferred_element_type=F32)
    act = (jax.nn.silu(gate) * up).astype(BF16)
    acc_ref[...] += jnp.dot(act, wd_ref[...], preferred_element_type=F32)

    @pl.when(j == pl.num_programs(1) - 1)
    def _():
        o_ref[...] = x1_ref[...] + _rms(acc_ref[...], gpost_ref[...])


def _ffn(h2, x1, w):
    t, d = x1.shape
    d_ff = w["w_gate"].shape[1]
    tm = _tile(t, 512)
    tf = _tile(d_ff, 512)
    return pl.pallas_call(
        _ffn_kernel,
        grid=(t // tm, d_ff // tf),
        in_specs=[
            pl.BlockSpec((tm, d), lambda i, j: (i, 0)),
            pl.BlockSpec((tm, d), lambda i, j: (i, 0)),
            pl.BlockSpec((d, tf), lambda i, j: (0, j)),
            pl.BlockSpec((d, tf), lambda i, j: (0, j)),
            pl.BlockSpec((tf, d), lambda i, j: (j, 0)),
            pl.BlockSpec((1, d), lambda i, j: (0, 0)),
        ],
        out_specs=pl.BlockSpec((tm, d), lambda i, j: (i, 0)),
        out_shape=jax.ShapeDtypeStruct((t, d), F32),
        scratch_shapes=[pltpu.VMEM((tm, d), F32)],
        compiler_params=_params(("arbitrary", "arbitrary")),
        name="ffn",
    )(h2, x1, w["w_gate"], w["w_up"], w["w_down"], w["g_post_ffn"])


def _rope_table(length):
    assert length % LANES == 0
    inv = 1.0 / (ROPE_THETA ** (jnp.arange(0, QK_ROPE, 2, dtype=F32) / QK_ROPE))
    hi = (jnp.arange(length // LANES, dtype=F32) * LANES)[:, None] * inv[None, :]
    lo = jnp.arange(LANES, dtype=F32)[:, None] * inv[None, :]
    ch, sh = jnp.cos(hi)[:, None, :], jnp.sin(hi)[:, None, :]
    cl, sl = jnp.cos(lo)[None, :, :], jnp.sin(lo)[None, :, :]
    cos = (ch * cl - sh * sl).reshape(length, -1)
    sin = (sh * cl + ch * sl).reshape(length, -1)
    return jnp.concatenate([cos, cos, -sin, sin], axis=-1)


def _swap_halves(wr):
    half = QK_ROPE // 2
    return jnp.concatenate([wr[..., half:], wr[..., :half]], axis=-1)


def _prepare_weights(g_pre_mix, w_in, g_q_a, w_q_b, g_kv_a, w_kv_b, lam_re, lam_im, log_dt,
                     b_re, b_im, c_re, c_im, ssm_d, w_glu, b_glu, g_out_attn, g_out_ssm, w_out,
                     g_post_mix, g_pre_ffn, w_gate, w_up, w_down, g_post_ffn):
    row = lambda g: g.astype(F32).reshape(1, -1)
    i1 = Q_LORA + KV_LORA
    i2 = i1 + QK_ROPE
    w_rope = w_in[:, i1:i2]
    w_in_ext = jnp.concatenate([w_in[:, :i1], w_in[:, i2:], w_rope, _swap_halves(w_rope)], axis=1)
    wq = w_q_b.reshape(Q_LORA, N_HEADS, QK_NOPE + QK_ROPE)
    wq_ext = jnp.concatenate([wq, _swap_halves(wq[..., QK_NOPE:])], axis=-1)
    wkv = w_kv_b.reshape(KV_LORA, N_HEADS, QK_NOPE + V_HEAD)
    n_a = N_HEADS * V_HEAD
    out = dict(
        n_u=w_in.shape[1] - i2,
        q_scale=float((QK_NOPE + QK_ROPE) ** -0.5 * math.log2(math.e)),
        g_pre_mix=row(g_pre_mix), w_in=w_in_ext.astype(BF16),
        g_q_a=row(g_q_a), w_qt=wq_ext.reshape(Q_LORA, N_HEADS * HEAD_PAD).T.astype(BF16),
        g_kv_a=row(g_kv_a),
        w_k=wkv[..., :QK_NOPE].reshape(KV_LORA, N_HEADS * QK_NOPE).astype(BF16),
        w_vt=wkv[..., QK_NOPE:].reshape(KV_LORA, N_HEADS * V_HEAD).T.astype(BF16),
        ssm_d=row(ssm_d), w_glu=w_glu.astype(BF16), b_glu=row(b_glu),
        g_out_attn=row(g_out_attn), g_out_ssm=row(g_out_ssm),
        w_out_a=w_out[:n_a].astype(BF16), w_out_s=w_out[n_a:].astype(BF16),
        g_post_mix=row(g_post_mix), g_pre_ffn=row(g_pre_ffn),
        w_gate=w_gate.astype(BF16), w_up=w_up.astype(BF16), w_down=w_down.astype(BF16),
        g_post_ffn=row(g_post_ffn),
    )
    out.update(_ssm_matrices(lam_re, lam_im, log_dt, b_re, b_im, c_re, c_im))
    return out


def _layer(x, w):
    batch, seq_len, d = x.shape
    x2d = x.reshape(batch * seq_len, d)
    cs = _rope_table(seq_len)
    qt, k, vt, u, uc = _in_proj(x2d, seq_len, w, cs, cs.T)
    a = _attention(qt, k, vt, batch, seq_len)
    ys = _ssm(uc, batch, seq_len, w)
    x1, h2 = _mix_out(ys, u, a, x2d, w)
    return _ffn(h2, x1, w).reshape(batch, seq_len, d)


def kernel(x_prompt, x_sample, g_pre_mix, w_in, g_q_a, w_q_b, g_kv_a, w_kv_b, ssm_lam_re, ssm_lam_im, ssm_log_dt, ssm_b_re, ssm_b_im, ssm_c_re, ssm_c_im, ssm_d, w_glu, b_glu, g_out_attn, g_out_ssm, w_out, g_post_mix, g_pre_ffn, w_gate, w_up, w_down, g_post_ffn):
    weights = (g_pre_mix, w_in, g_q_a, w_q_b, g_kv_a, w_kv_b, ssm_lam_re, ssm_lam_im, ssm_log_dt,
               ssm_b_re, ssm_b_im, ssm_c_re, ssm_c_im, ssm_d, w_glu, b_glu, g_out_attn, g_out_ssm,
               w_out, g_post_mix, g_pre_ffn, w_gate, w_up, w_down, g_post_ffn)
    depth = g_pre_mix.shape[0]
    for layer in range(depth):
        w = _prepare_weights(*[p[layer] for p in weights])
        x_prompt = _layer(x_prompt, w)
        x_sample = _layer(x_sample, w)
    return (x_prompt, x_sample)
```

```python
import functools
import math

import jax
import jax.numpy as jnp
from jax import lax
from jax.experimental import pallas as pl
from jax.experimental.pallas import tpu as pltpu

F32 = jnp.float32
BF16 = jnp.bfloat16

RMS_EPS = 1e-6
ROPE_THETA = 10000.0
LANES = 128
SUBLANES = 8
VMEM_LIMIT_BYTES = 56 * 1024 * 1024

N_HEADS = 8
QK_NOPE = 128
QK_ROPE = 64
V_HEAD = 128
HEAD_PAD = 256
Q_LORA = 512
KV_LORA = 256
ACC_ROWS = V_HEAD + 16
ATTN_UNROLL = 2
SSM_GROUP = 16
SSM_STATE = 64
SSM_CHUNK = LANES // SSM_GROUP
SSM_PITCH_PAD = 8


def _rms(x, g):
    return x * lax.rsqrt(jnp.mean(x * x, axis=-1, keepdims=True) + RMS_EPS) * g


def _params(sem):
    return pltpu.CompilerParams(dimension_semantics=sem, vmem_limit_bytes=VMEM_LIMIT_BYTES)


def _tile(n, target):
    t = min(n, target)
    while n % t:
        t //= 2
    return t


def _block_transpose(a):
    lane = lax.broadcasted_iota(jnp.int32, a[0].shape, 1)
    for d in (4, 2, 1):
        shift = SSM_GROUP * d
        upper = (lane & shift) != 0
        new = list(a)
        for i in range(len(a)):
            if i & d:
                continue
            x, y = a[i], a[i | d]
            new[i] = jnp.where(upper, pltpu.roll(y, shift, 1), x)
            new[i | d] = jnp.where(upper, y, pltpu.roll(x, LANES - shift, 1))
        a = new
    return a


def _const_spec(shape):
    nd = len(shape)
    return pl.BlockSpec(shape, lambda *_: (0,) * nd, pipeline_mode=pl.Buffered(1))


def _in_proj_kernel(x_ref, gpre_ref, win_ref, gq_ref, wqt_ref, gkv_ref, wk_ref, wvt_ref, cs_ref, cst_ref,
                    qt_ref, k_ref, vt_ref, u_ref, uc_ref, *, q_scale):
    x = x_ref[...]
    h = _rms(x, gpre_ref[...])
    z = jnp.dot(h.astype(BF16), win_ref[...], preferred_element_type=F32)
    c_q = z[:, :Q_LORA]
    c_kv = z[:, Q_LORA:Q_LORA + KV_LORA]
    n_blk = u_ref.shape[0]
    u0 = Q_LORA + KV_LORA
    for blk in range(n_blk):
        u_ref[blk] = z[:, u0 + blk * LANES:u0 + (blk + 1) * LANES]
    k_rope = z[:, u0 + n_blk * LANES:]
    cs = cs_ref[...]

    def rotate(t):
        t = t * cs
        return t + pltpu.roll(t, QK_ROPE, 1)

    cqn = _rms(c_q, gq_ref[...]).astype(BF16)
    qt = lax.dot_general(wqt_ref[...], cqn, (((1,), (1,)), ((), ())),
                         preferred_element_type=F32) * q_scale
    cst = cst_ref[...]
    for hd in range(N_HEADS):
        lo = hd * HEAD_PAD
        qt_ref[lo:lo + QK_NOPE] = qt[lo:lo + QK_NOPE].astype(BF16)
        t = qt[lo + QK_NOPE:lo + HEAD_PAD] * cst
        roped = (t[:QK_ROPE] + t[QK_ROPE:]).astype(BF16)
        qt_ref[lo + QK_NOPE:lo + QK_NOPE + QK_ROPE] = roped
        qt_ref[lo + QK_NOPE + QK_ROPE:lo + HEAD_PAD] = roped

    ckvn = _rms(c_kv, gkv_ref[...]).astype(BF16)
    k_nope = jnp.dot(ckvn, wk_ref[...], preferred_element_type=F32)
    vt = lax.dot_general(wvt_ref[...], ckvn, (((1,), (1,)), ((), ())),
                         preferred_element_type=F32).astype(BF16)
    ones = jnp.ones((ACC_ROWS - V_HEAD, vt.shape[1]), BF16)
    for hd in range(N_HEADS):
        vt_ref[hd * ACC_ROWS:hd * ACC_ROWS + V_HEAD] = vt[hd * V_HEAD:(hd + 1) * V_HEAD]
        vt_ref[hd * ACC_ROWS + V_HEAD:(hd + 1) * ACC_ROWS] = ones
    lane = lax.broadcasted_iota(jnp.int32, k_rope.shape, 1)
    kr = jnp.where(lane < QK_ROPE, rotate(k_rope), 0.0).astype(BF16)
    for hd in range(N_HEADS):
        lo = hd * HEAD_PAD
        k_ref[:, lo:lo + QK_NOPE] = k_nope[:, hd * QK_NOPE:(hd + 1) * QK_NOPE].astype(BF16)
        k_ref[:, lo + QK_NOPE:lo + HEAD_PAD] = kr

    n_j = x.shape[0] // SSM_CHUNK
    for blk in range(n_blk):
        steps = [u_ref[blk, pl.ds(t, n_j, stride=SSM_CHUNK), :] for t in range(SSM_CHUNK)]
        groups = _block_transpose(steps)
        for gl in range(SUBLANES):
            uc_ref[blk * SUBLANES + gl] = groups[gl].astype(BF16)


def _in_proj(x2d, seq_len, w, cs, cst):
    t, d = x2d.shape
    tm = _tile(seq_len, 512)
    n_blk = w["n_u"] // LANES
    n_g = w["n_u"] // SSM_GROUP
    kern = functools.partial(_in_proj_kernel, q_scale=w["q_scale"])
    tiles_per_seq = seq_len // tm
    row = lambda i: (i, 0)
    return pl.pallas_call(
        kern,
        grid=(t // tm,),
        in_specs=[
            pl.BlockSpec((tm, d), row),
            _const_spec((1, d)),
            _const_spec(w["w_in"].shape),
            _const_spec((1, Q_LORA)),
            _const_spec(w["w_qt"].shape),
            _const_spec((1, KV_LORA)),
            _const_spec(w["w_k"].shape),
            _const_spec(w["w_vt"].shape),
            pl.BlockSpec((tm, LANES), lambda i: (i % tiles_per_seq, 0)),
            pl.BlockSpec((LANES, tm), lambda i: (0, i % tiles_per_seq)),
        ],
        out_specs=[
            pl.BlockSpec((N_HEADS * HEAD_PAD, tm), lambda i: (0, i)),
            pl.BlockSpec((tm, N_HEADS * HEAD_PAD), row),
            pl.BlockSpec((N_HEADS * ACC_ROWS, tm), lambda i: (0, i)),
            pl.BlockSpec((n_blk, tm, LANES), lambda i: (0, i, 0)),
            pl.BlockSpec((n_g, tm // SSM_CHUNK, LANES), lambda i: (0, i, 0)),
        ],
        out_shape=[
            jax.ShapeDtypeStruct((N_HEADS * HEAD_PAD, t), BF16),
            jax.ShapeDtypeStruct((t, N_HEADS * HEAD_PAD), BF16),
            jax.ShapeDtypeStruct((N_HEADS * ACC_ROWS, t), BF16),
            jax.ShapeDtypeStruct((n_blk, t, LANES), F32),
            jax.ShapeDtypeStruct((n_g, t // SSM_CHUNK, LANES), BF16),
        ],
        compiler_params=_params(("arbitrary",)),
        name="in_proj",
    )(x2d, w["g_pre_mix"], w["w_in"], w["g_q_a"], w["w_qt"], w["g_kv_a"], w["w_k"], w["w_vt"], cs, cst)


def _flash_kernel(qt_ref, k_ref, vt_ref, o_ref, s_ref, mx_ref, m_ref, acc_ref, *, tk):
    n_k = k_ref.shape[1] // tk

    def scores(i, slot):
        start = pl.multiple_of(i * tk, tk)
        s = jnp.dot(k_ref[0, pl.ds(start, tk), :], qt_ref[...],
                    preferred_element_type=F32)
        s_ref[slot] = s
        x = s.reshape(tk // SUBLANES, SUBLANES, s.shape[-1])
        while x.shape[0] > 1:
            half = x.shape[0] // 2
            x = jnp.maximum(x[:half], x[half:])
        mx_ref[slot] = x[0]

    def update(i, slot):
        m = m_ref[...]
        m_new = jnp.maximum(m, jnp.max(mx_ref[slot], axis=0, keepdims=True))
        alpha = jnp.exp2(m - m_new)
        p = jnp.exp2(s_ref[slot] - m_new).astype(BF16)
        start = pl.multiple_of(i * tk, tk)
        pv = jnp.dot(vt_ref[:, pl.ds(start, tk)], p, preferred_element_type=F32)
        acc_ref[...] = alpha * acc_ref[...] + pv
        m_ref[...] = m_new

    m_ref[...] = jnp.full_like(m_ref, -jnp.inf)
    acc_ref[...] = jnp.zeros_like(acc_ref)
    scores(0, 0)

    def body(j, _):
        for t in range(ATTN_UNROLL):
            scores(j * ATTN_UNROLL + t + 1, (t + 1) % 2)
            update(j * ATTN_UNROLL + t, t % 2)
        return 0

    n_loop = (n_k - 1) // ATTN_UNROLL
    lax.fori_loop(0, n_loop, body, 0)
    for i in range(n_loop * ATTN_UNROLL, n_k):
        if i + 1 < n_k:
            scores(i + 1, (i + 1) % 2)
        update(i, i % 2)
    acc = acc_ref[...]
    o_ref[0] = (acc[:V_HEAD] / acc[V_HEAD:V_HEAD + 1]).T.astype(o_ref.dtype)


def _attention(qt, k, vt, batch, seq_len):
    k = k.reshape(batch, seq_len, N_HEADS * HEAD_PAD)
    tq = _tile(seq_len, 2048)
    tk = _tile(seq_len, 1024)
    out = pl.pallas_call(
        functools.partial(_flash_kernel, tk=tk),
        grid=(batch, N_HEADS, seq_len // tq),
        in_specs=[
            pl.BlockSpec((HEAD_PAD, tq), lambda b, h, i: (h, b * (seq_len // tq) + i)),
            pl.BlockSpec((1, seq_len, HEAD_PAD), lambda b, h, i: (b, 0, h)),
            pl.BlockSpec((ACC_ROWS, seq_len), lambda b, h, i: (h, b)),
        ],
        out_specs=pl.BlockSpec((1, tq, V_HEAD), lambda b, h, i: (b, i, h)),
        out_shape=jax.ShapeDtypeStruct((batch, seq_len, N_HEADS * V_HEAD), BF16),
        scratch_shapes=[pltpu.VMEM((2, tk, tq), F32), pltpu.VMEM((2, SUBLANES, tq), F32),
                        pltpu.VMEM((1, tq), F32), pltpu.VMEM((ACC_ROWS, tq), F32)],
        compiler_params=_params(("arbitrary", "arbitrary", "arbitrary")),
        name="attention",
    )(qt, k, vt)
    return out.reshape(batch * seq_len, N_HEADS * V_HEAD)


def _ssm_matrices(lam_re, lam_im, log_dt, b_re, b_im, c_re, c_im):
    t_c = SSM_CHUNK
    n_g = lam_re.shape[1]
    w_in, w_state, a_r, a_s = [], [], [], []
    toep = 0.0
    for d in range(2):
        dt = jnp.exp(log_dt[d].astype(F32))[:, None]
        lam = lax.complex(lam_re[d].astype(F32), lam_im[d].astype(F32))
        a = jnp.exp(lam * dt)
        coef = (a - 1.0) / lam
        bt = lax.complex(b_re[d].astype(F32), b_im[d].astype(F32)) * coef[:, :, None]
        cc = lax.complex(c_re[d].astype(F32), c_im[d].astype(F32))
        steps = jnp.arange(t_c + 1, dtype=F32)
        apow = jnp.exp(lam[None] * dt[None] * steps[:, None, None])
        in_pow = apow[:t_c][::-1] if d == 0 else apow[:t_c]
        win = jnp.einsum("sgp,gpc->gscp", in_pow, bt, precision=lax.Precision.HIGHEST).reshape(n_g, LANES, SSM_STATE)
        w_in.append(jnp.concatenate([win.real, win.imag, win.imag, win.real], axis=-1))
        out_pow = apow[1:] if d == 0 else apow[1:][::-1]
        wst = jnp.einsum("gcp,tgp->gptc", cc, out_pow, precision=lax.Precision.HIGHEST).reshape(n_g, SSM_STATE, LANES)
        w_state.append(jnp.concatenate([wst.real, -wst.imag], axis=1))
        kern = jnp.einsum("gcp,kgp,gpe->gekc", cc, apow[:t_c], bt,
                          precision=lax.Precision.HIGHEST).real.reshape(n_g, SSM_GROUP, LANES)
        if d == 0:
            rows = [jnp.pad(kern[..., :LANES - SSM_GROUP * s], ((0, 0), (0, 0), (SSM_GROUP * s, 0)))
                    for s in range(t_c)]
        else:
            rev = kern.reshape(n_g, SSM_GROUP, t_c, SSM_GROUP)[:, :, ::-1].reshape(kern.shape)
            rows = [jnp.pad(rev[..., SSM_GROUP * (t_c - 1 - s):],
                            ((0, 0), (0, 0), (0, SSM_GROUP * (t_c - 1 - s)))) for s in range(t_c)]
        toep = toep + jnp.stack(rows, axis=1).reshape(n_g, LANES, LANES)
        a_t = apow[t_c]
        a_r.append(jnp.concatenate([a_t.real, a_t.real], axis=-1))
        a_s.append(jnp.concatenate([-a_t.imag, a_t.imag], axis=-1))
    w_out = jnp.concatenate([toep, w_state[0], w_state[1]], axis=1)
    return dict(w_in_f=w_in[0].astype(BF16), w_in_b=w_in[1].astype(BF16), w_out=w_out.astype(BF16),
                ar_f=a_r[0], as_f=a_s[0], ar_b=a_r[1], as_b=a_s[1])


def _ssm_state_kernel(uf_ref, ub_ref, wf_ref, wb_ref, arf_ref, asf_ref, arb_ref, asb_ref,
                      xf_ref, xb_ref, vf, vft, vb, vbt, carry, *, tj, pitch):
    n_g = uf_ref.shape[0]
    n_blk = n_g // SUBLANES

    @pl.when(pl.program_id(1) == 0)
    def _():
        carry[...] = jnp.zeros_like(carry)

    def project(blk, _):
        for gl in range(SUBLANES):
            g = blk * SUBLANES + gl
            row = pl.multiple_of(g * pitch, SUBLANES)
            pf = jnp.dot(uf_ref[g], wf_ref[g], preferred_element_type=F32)
            vf[pl.ds(row, tj), :] = pf[:, :LANES]
            vft[pl.ds(row, tj), :] = pf[:, LANES:]
            pb = jnp.dot(ub_ref[g], wb_ref[g], preferred_element_type=F32)
            vb[pl.ds(row, tj), :] = pb[:, :LANES]
            vbt[pl.ds(row, tj), :] = pb[:, LANES:]
        return 0

    lax.fori_loop(0, n_blk, project, 0)

    def scan(j, state):
        jb = tj - 1 - j
        new = []
        for blk in range(n_blk):
            base = blk * SUBLANES * pitch
            gs = pl.ds(blk * SUBLANES, SUBLANES)
            for d, (v, vt, ar_ref, as_ref, jj) in enumerate(
                    ((vf, vft, arf_ref, asf_ref, j), (vb, vbt, arb_ref, asb_ref, jb))):
                x, xt = state[(blk * 2 + d) * 2], state[(blk * 2 + d) * 2 + 1]
                rows = pl.ds(base + jj, SUBLANES, stride=pitch)
                vx = v[rows, :]
                vxt = vt[rows, :]
                v[rows, :] = x
                ar = ar_ref[gs, :]
                a_s = as_ref[gs, :]
                new.append(ar * x + a_s * xt + vx)
                new.append(ar * xt - a_s * x + vxt)
        return tuple(new)

    init = tuple(carry[i] for i in range(4 * n_blk))
    final = lax.fori_loop(0, tj, scan, init)
    for i in range(4 * n_blk):
        carry[i] = final[i]

    def emit(g, _):
        row = pl.multiple_of(g * pitch, SUBLANES)
        xf_ref[g] = vf[pl.ds(row, tj), :].astype(BF16)
        xb_ref[g] = vb[pl.ds(row, tj), :].astype(BF16)
        return 0

    lax.fori_loop(0, n_g, emit, 0, unroll=4)


def _ssm_out_kernel(u_ref, xf_ref, xb_ref, w_ref, y_ref, *, tj):
    n_blk = y_ref.shape[0]

    def body(blk, _):
        groups = []
        for gl in range(SUBLANES):
            g = blk * SUBLANES + gl
            lhs = jnp.concatenate([u_ref[g], xf_ref[g], xb_ref[g]], axis=1)
            groups.append(jnp.dot(lhs, w_ref[g], preferred_element_type=F32))
        steps = _block_transpose(groups)
        for t in range(SSM_CHUNK):
            y_ref[blk, pl.ds(t, tj, stride=SSM_CHUNK), :] = steps[t]
        return 0

    lax.fori_loop(0, n_blk, body, 0)


def _ssm(uc, batch, seq_len, w):
    n_g = uc.shape[0]
    n_j = seq_len // SSM_CHUNK
    tj = _tile(n_j, 128)
    n_t = n_j // tj
    pitch = tj + SSM_PITCH_PAD
    blk = (n_g, tj, LANES)
    fwd = lambda b, i: (0, b * n_t + i, 0)
    bwd = lambda b, i: (0, b * n_t + n_t - 1 - i, 0)
    scan_buf = pltpu.VMEM((n_g * pitch, LANES), F32)
    xf, xb = pl.pallas_call(
        functools.partial(_ssm_state_kernel, tj=tj, pitch=pitch),
        grid=(batch, n_t),
        in_specs=[
            pl.BlockSpec(blk, fwd),
            pl.BlockSpec(blk, bwd),
            _const_spec(w["w_in_f"].shape),
            _const_spec(w["w_in_b"].shape),
            _const_spec(w["ar_f"].shape),
            _const_spec(w["as_f"].shape),
            _const_spec(w["ar_b"].shape),
            _const_spec(w["as_b"].shape),
        ],
        out_specs=[pl.BlockSpec(blk, fwd), pl.BlockSpec(blk, bwd)],
        out_shape=[jax.ShapeDtypeStruct(uc.shape, BF16)] * 2,
        scratch_shapes=[scan_buf, scan_buf, scan_buf, scan_buf,
                        pltpu.VMEM((4 * (n_g // SUBLANES), SUBLANES, LANES), F32)],
        compiler_params=_params(("arbitrary", "arbitrary")),
        name="ssm_state",
    )(uc, uc, w["w_in_f"], w["w_in_b"], w["ar_f"], w["as_f"], w["ar_b"], w["as_b"])
    n_blk = n_g // SUBLANES
    return pl.pallas_call(
        functools.partial(_ssm_out_kernel, tj=tj),
        grid=(batch, n_t),
        in_specs=[pl.BlockSpec(blk, fwd), pl.BlockSpec(blk, fwd), pl.BlockSpec(blk, fwd),
                  _const_spec(w["w_out"].shape)],
        out_specs=pl.BlockSpec((n_blk, tj * SSM_CHUNK, LANES), fwd),
        out_shape=jax.ShapeDtypeStruct((n_blk, batch * seq_len, LANES), F32),
        compiler_params=_params(("arbitrary", "arbitrary")),
        name="ssm_out",
    )(uc, xf, xb, w["w_out"])


def _mix_out_kernel(ys_ref, u_ref, a_ref, x_ref, d_ref, wglu_ref, bglu_ref, ga_ref, gs_ref,
                    woa_ref, wos_ref, gpost_ref, gffn_ref, x1_ref, h2_ref):
    n_blk = ys_ref.shape[0]
    ys = jnp.concatenate([ys_ref[b] for b in range(n_blk)], axis=1)
    u = jnp.concatenate([u_ref[b] for b in range(n_blk)], axis=1)
    y = jax.nn.gelu(ys + d_ref[...] * u)
    gate = jax.nn.sigmoid(jnp.dot(y.astype(BF16), wglu_ref[...], preferred_element_type=F32)
                          + bglu_ref[...])
    s = y * gate
    an = _rms(a_ref[...].astype(F32), ga_ref[...]).astype(BF16)
    sn = _rms(s, gs_ref[...]).astype(BF16)
    m = (jnp.dot(an, woa_ref[...], preferred_element_type=F32)
         + jnp.dot(sn, wos_ref[...], preferred_element_type=F32))
    x1 = x_ref[...] + _rms(m, gpost_ref[...])
    x1_ref[...] = x1
    h2_ref[...] = _rms(x1, gffn_ref[...]).astype(BF16)


def _mix_out(ys, u, a, x2d, w):
    t, d = x2d.shape
    n_a = a.shape[1]
    n_blk = ys.shape[0]
    n_s = n_blk * LANES
    tm = _tile(t, 512)
    row = lambda i: (i, 0)
    return pl.pallas_call(
        _mix_out_kernel,
        grid=(t // tm,),
        in_specs=[
            pl.BlockSpec((n_blk, tm, LANES), lambda i: (0, i, 0)),
            pl.BlockSpec((n_blk, tm, LANES), lambda i: (0, i, 0)),
            pl.BlockSpec((tm, n_a), row),
            pl.BlockSpec((tm, d), row),
            _const_spec((1, n_s)),
            _const_spec(w["w_glu"].shape),
            _const_spec((1, n_s)),
            _const_spec((1, n_a)),
            _const_spec((1, n_s)),
            _const_spec(w["w_out_a"].shape),
            _const_spec(w["w_out_s"].shape),
            _const_spec((1, d)),
            _const_spec((1, d)),
        ],
        out_specs=[pl.BlockSpec((tm, d), row), pl.BlockSpec((tm, d), row)],
        out_shape=[jax.ShapeDtypeStruct((t, d), F32), jax.ShapeDtypeStruct((t, d), BF16)],
        compiler_params=_params(("arbitrary",)),
        name="mix_out",
    )(ys, u, a, x2d, w["ssm_d"], w["w_glu"], w["b_glu"], w["g_out_attn"], w["g_out_ssm"],
      w["w_out_a"], w["w_out_s"], w["g_post_mix"], w["g_pre_ffn"])


def _ffn_kernel(h_ref, x1_ref, wg_ref, wu_ref, wd_ref, gpost_ref, o_ref, acc_ref):
    j = pl.program_id(1)

    @pl.when(j == 0)
    def _():
        acc_ref[...] = jnp.zeros_like(acc_ref)

    h = h_ref[...]
    gate = jnp.dot(h, wg_ref[...], preferred_element_type=F32)
    up = jnp.dot(h, wu_ref[...], preferred_element_type=F32)
    act = (jax.nn.silu(gate) * up).astype(BF16)
    acc_ref[...] += jnp.dot(act, wd_ref[...], preferred_element_type=F32)

    @pl.when(j == pl.num_programs(1) - 1)
    def _():
        o_ref[...] = x1_ref[...] + _rms(acc_ref[...], gpost_ref[...])


def _ffn(h2, x1, w):
    t, d = x1.shape
    d_ff = w["w_gate"].shape[1]
    tm = _tile(t, 512)
    tf = _tile(d_ff, 512)
    return pl.pallas_call(
        _ffn_kernel,
        grid=(t // tm, d_ff // tf),
        in_specs=[
            pl.BlockSpec((tm, d), lambda i, j: (i, 0)),
            pl.BlockSpec((tm, d), lambda i, j: (i, 0)),
            pl.BlockSpec((d, tf), lambda i, j: (0, j)),
            pl.BlockSpec((d, tf), lambda i, j: (0, j)),
            pl.BlockSpec((tf, d), lambda i, j: (j, 0)),
            pl.BlockSpec((1, d), lambda i, j: (0, 0)),
        ],
        out_specs=pl.BlockSpec((tm, d), lambda i, j: (i, 0)),
        out_shape=jax.ShapeDtypeStruct((t, d), F32),
        scratch_shapes=[pltpu.VMEM((tm, d), F32)],
        compiler_params=_params(("arbitrary", "arbitrary")),
        name="ffn",
    )(h2, x1, w["w_gate"], w["w_up"], w["w_down"], w["g_post_ffn"])


def _rope_table(length):
    assert length % LANES == 0
    inv = 1.0 / (ROPE_THETA ** (jnp.arange(0, QK_ROPE, 2, dtype=F32) / QK_ROPE))
    hi = (jnp.arange(length // LANES, dtype=F32) * LANES)[:, None] * inv[None, :]
    lo = jnp.arange(LANES, dtype=F32)[:, None] * inv[None, :]
    ch, sh = jnp.cos(hi)[:, None, :], jnp.sin(hi)[:, None, :]
    cl, sl = jnp.cos(lo)[None, :, :], jnp.sin(lo)[None, :, :]
    cos = (ch * cl - sh * sl).reshape(length, -1)
    sin = (sh * cl + ch * sl).reshape(length, -1)
    return jnp.concatenate([cos, cos, -sin, sin], axis=-1)


def _swap_halves(wr):
    half = QK_ROPE // 2
    return jnp.concatenate([wr[..., half:], wr[..., :half]], axis=-1)


def _prepare_weights(g_pre_mix, w_in, g_q_a, w_q_b, g_kv_a, w_kv_b, lam_re, lam_im, log_dt,
                     b_re, b_im, c_re, c_im, ssm_d, w_glu, b_glu, g_out_attn, g_out_ssm, w_out,
                     g_post_mix, g_pre_ffn, w_gate, w_up, w_down, g_post_ffn):
    row = lambda g: g.astype(F32).reshape(1, -1)
    i1 = Q_LORA + KV_LORA
    i2 = i1 + QK_ROPE
    w_rope = w_in[:, i1:i2]
    w_in_ext = jnp.concatenate([w_in[:, :i1], w_in[:, i2:], w_rope, _swap_halves(w_rope)], axis=1)
    wq = w_q_b.reshape(Q_LORA, N_HEADS, QK_NOPE + QK_ROPE)
    wq_ext = jnp.concatenate([wq, _swap_halves(wq[..., QK_NOPE:])], axis=-1)
    wkv = w_kv_b.reshape(KV_LORA, N_HEADS, QK_NOPE + V_HEAD)
    n_a = N_HEADS * V_HEAD
    out = dict(
        n_u=w_in.shape[1] - i2,
        q_scale=float((QK_NOPE + QK_ROPE) ** -0.5 * math.log2(math.e)),
        g_pre_mix=row(g_pre_mix), w_in=w_in_ext.astype(BF16),
        g_q_a=row(g_q_a), w_qt=wq_ext.reshape(Q_LORA, N_HEADS * HEAD_PAD).T.astype(BF16),
        g_kv_a=row(g_kv_a),
        w_k=wkv[..., :QK_NOPE].reshape(KV_LORA, N_HEADS * QK_NOPE).astype(BF16),
        w_vt=wkv[..., QK_NOPE:].reshape(KV_LORA, N_HEADS * V_HEAD).T.astype(BF16),
        ssm_d=row(ssm_d), w_glu=w_glu.astype(BF16), b_glu=row(b_glu),
        g_out_attn=row(g_out_attn), g_out_ssm=row(g_out_ssm),
        w_out_a=w_out[:n_a].astype(BF16), w_out_s=w_out[n_a:].astype(BF16),
        g_post_mix=row(g_post_mix), g_pre_ffn=row(g_pre_ffn),
        w_gate=w_gate.astype(BF16), w_up=w_up.astype(BF16), w_down=w_down.astype(BF16),
        g_post_ffn=row(g_post_ffn),
    )
    out.update(_ssm_matrices(lam_re, lam_im, log_dt, b_re, b_im, c_re, c_im))
    return out


def _layer(x, w):
    batch, seq_len, d = x.shape
    x2d = x.reshape(batch * seq_len, d)
    cs = _rope_table(seq_len)
    qt, k, vt, u, uc = _in_proj(x2d, seq_len, w, cs, cs.T)
    a = _attention(qt, k, vt, batch, seq_len)
    ys = _ssm(uc, batch, seq_len, w)
    x1, h2 = _mix_out(ys, u, a, x2d, w)
    return _ffn(h2, x1, w).reshape(batch, seq_len, d)


def kernel(x_prompt, x_sample, g_pre_mix, w_in, g_q_a, w_q_b, g_kv_a, w_kv_b, ssm_lam_re, ssm_lam_im, ssm_log_dt, ssm_b_re, ssm_b_im, ssm_c_re, ssm_c_im, ssm_d, w_glu, b_glu, g_out_attn, g_out_ssm, w_out, g_post_mix, g_pre_ffn, w_gate, w_up, w_down, g_post_ffn):
    weights = (g_pre_mix, w_in, g_q_a, w_q_b, g_kv_a, w_kv_b, ssm_lam_re, ssm_lam_im, ssm_log_dt,
               ssm_b_re, ssm_b_im, ssm_c_re, ssm_c_im, ssm_d, w_glu, b_glu, g_out_attn, g_out_ssm,
               w_out, g_post_mix, g_pre_ffn, w_gate, w_up, w_down, g_post_ffn)
    depth = g_pre_mix.shape[0]
    for layer in range(depth):
        w = _prepare_weights(*[p[layer] for p in weights])
        x_prompt = _layer(x_prompt, w)
        x_sample = _layer(x_sample, w)
    return (x_prompt, x_sample)
```

```python
import functools
import math

import jax
import jax.numpy as jnp
import numpy as np
from jax import lax
from jax.experimental import pallas as pl
from jax.experimental.pallas import tpu as pltpu

F32 = jnp.float32
BF16 = jnp.bfloat16

RMS_EPS = 1e-6
ROPE_THETA = 10000.0
LANES = 128
SUBLANES = 8
VMEM_LIMIT_BYTES = 56 * 1024 * 1024

N_HEADS = 8
QK_NOPE = 128
QK_ROPE = 64
V_HEAD = 128
HEAD_PAD = 256
Q_LORA = 512
KV_LORA = 256
ACC_ROWS = V_HEAD + 16
ATTN_UNROLL = 2
SSM_GROUP = 16
SSM_STATE = 64
SSM_CHUNK = LANES // SSM_GROUP
SSM_PITCH_PAD = 8


def _rms(x, g):
    return x * lax.rsqrt(jnp.mean(x * x, axis=-1, keepdims=True) + RMS_EPS) * g


def _params(sem):
    return pltpu.CompilerParams(dimension_semantics=sem, vmem_limit_bytes=VMEM_LIMIT_BYTES)


def _tile(n, target):
    t = min(n, target)
    while n % t:
        t //= 2
    return t


def _block_transpose(a):
    lane = lax.broadcasted_iota(jnp.int32, a[0].shape, 1)
    for d in (4, 2, 1):
        shift = SSM_GROUP * d
        upper = (lane & shift) != 0
        new = list(a)
        for i in range(len(a)):
            if i & d:
                continue
            x, y = a[i], a[i | d]
            new[i] = jnp.where(upper, pltpu.roll(y, shift, 1), x)
            new[i | d] = jnp.where(upper, y, pltpu.roll(x, LANES - shift, 1))
        a = new
    return a


def _const_spec(shape):
    nd = len(shape)
    return pl.BlockSpec(shape, lambda *_: (0,) * nd, pipeline_mode=pl.Buffered(1))


def _in_proj_kernel(x_ref, gpre_ref, win_ref, gq_ref, wqt_ref, gkv_ref, wk_ref, wvt_ref, cs_ref, cst_ref,
                    qt_ref, k_ref, vt_ref, u_ref, uc_ref, *, q_scale):
    x = x_ref[...]
    h = _rms(x, gpre_ref[...])
    z = jnp.dot(h.astype(BF16), win_ref[...], preferred_element_type=F32)
    c_q = z[:, :Q_LORA]
    c_kv = z[:, Q_LORA:Q_LORA + KV_LORA]
    n_blk = u_ref.shape[0]
    u0 = Q_LORA + KV_LORA
    for blk in range(n_blk):
        u_ref[blk] = z[:, u0 + blk * LANES:u0 + (blk + 1) * LANES]
    k_rope = z[:, u0 + n_blk * LANES:]
    cs = cs_ref[...]

    def rotate(t):
        t = t * cs
        return t + pltpu.roll(t, QK_ROPE, 1)

    cqn = _rms(c_q, gq_ref[...]).astype(BF16)
    qt = lax.dot_general(wqt_ref[...], cqn, (((1,), (1,)), ((), ())),
                         preferred_element_type=F32) * q_scale
    cst = cst_ref[...]
    for hd in range(N_HEADS):
        lo = hd * HEAD_PAD
        qt_ref[lo:lo + QK_NOPE] = qt[lo:lo + QK_NOPE].astype(BF16)
        t = qt[lo + QK_NOPE:lo + HEAD_PAD] * cst
        roped = (t[:QK_ROPE] + t[QK_ROPE:]).astype(BF16)
        qt_ref[lo + QK_NOPE:lo + QK_NOPE + QK_ROPE] = roped
        qt_ref[lo + QK_NOPE + QK_ROPE:lo + HEAD_PAD] = roped

    ckvn = _rms(c_kv, gkv_ref[...]).astype(BF16)
    k_nope = jnp.dot(ckvn, wk_ref[...], preferred_element_type=F32)
    vt = lax.dot_general(wvt_ref[...], ckvn, (((1,), (1,)), ((), ())),
                         preferred_element_type=F32).astype(BF16)
    ones = jnp.ones((ACC_ROWS - V_HEAD, vt.shape[1]), BF16)
    for hd in range(N_HEADS):
        vt_ref[hd * ACC_ROWS:hd * ACC_ROWS + V_HEAD] = vt[hd * V_HEAD:(hd + 1) * V_HEAD]
        vt_ref[hd * ACC_ROWS + V_HEAD:(hd + 1) * ACC_ROWS] = ones
    lane = lax.broadcasted_iota(jnp.int32, k_rope.shape, 1)
    kr = jnp.where(lane < QK_ROPE, rotate(k_rope), 0.0).astype(BF16)
    for hd in range(N_HEADS):
        lo = hd * HEAD_PAD
        k_ref[:, lo:lo + QK_NOPE] = k_nope[:, hd * QK_NOPE:(hd + 1) * QK_NOPE].astype(BF16)
        k_ref[:, lo + QK_NOPE:lo + HEAD_PAD] = kr

    n_j = x.shape[0] // SSM_CHUNK
    for blk in range(n_blk):
        steps = [u_ref[blk, pl.ds(t, n_j, stride=SSM_CHUNK), :] for t in range(SSM_CHUNK)]
        groups = _block_transpose(steps)
        for gl in range(SUBLANES):
            uc_ref[blk * SUBLANES + gl] = groups[gl].astype(BF16)


def _in_proj(x2d, seq_len, w, cs, cst):
    t, d = x2d.shape
    tm = _tile(seq_len, 512)
    n_blk = w["n_u"] // LANES
    n_g = w["n_u"] // SSM_GROUP
    kern = functools.partial(_in_proj_kernel, q_scale=w["q_scale"])
    tiles_per_seq = seq_len // tm
    row = lambda i: (i, 0)
    return pl.pallas_call(
        kern,
        grid=(t // tm,),
        in_specs=[
            pl.BlockSpec((tm, d), row),
            _const_spec((1, d)),
            _const_spec(w["w_in"].shape),
            _const_spec((1, Q_LORA)),
            _const_spec(w["w_qt"].shape),
            _const_spec((1, KV_LORA)),
            _const_spec(w["w_k"].shape),
            _const_spec(w["w_vt"].shape),
            pl.BlockSpec((tm, LANES), lambda i: (i % tiles_per_seq, 0)),
            pl.BlockSpec((LANES, tm), lambda i: (0, i % tiles_per_seq)),
        ],
        out_specs=[
            pl.BlockSpec((N_HEADS * HEAD_PAD, tm), lambda i: (0, i)),
            pl.BlockSpec((tm, N_HEADS * HEAD_PAD), row),
            pl.BlockSpec((N_HEADS * ACC_ROWS, tm), lambda i: (0, i)),
            pl.BlockSpec((n_blk, tm, LANES), lambda i: (0, i, 0)),
            pl.BlockSpec((n_g, tm // SSM_CHUNK, LANES), lambda i: (0, i, 0)),
        ],
        out_shape=[
            jax.ShapeDtypeStruct((N_HEADS * HEAD_PAD, t), BF16),
            jax.ShapeDtypeStruct((t, N_HEADS * HEAD_PAD), BF16),
            jax.ShapeDtypeStruct((N_HEADS * ACC_ROWS, t), BF16),
            jax.ShapeDtypeStruct((n_blk, t, LANES), F32),
            jax.ShapeDtypeStruct((n_g, t // SSM_CHUNK, LANES), BF16),
        ],
        compiler_params=_params(("arbitrary",)),
        name="in_proj",
    )(x2d, w["g_pre_mix"], w["w_in"], w["g_q_a"], w["w_qt"], w["g_kv_a"], w["w_k"], w["w_vt"], cs, cst)


def _flash_kernel(qt_ref, k_ref, vt_ref, o_ref, s_ref, mx_ref, m_ref, acc_ref, *, tk):
    n_k = k_ref.shape[1] // tk

    def scores(i, slot):
        start = pl.multiple_of(i * tk, tk)
        s = jnp.dot(k_ref[0, pl.ds(start, tk), :], qt_ref[...],
                    preferred_element_type=F32)
        s_ref[slot] = s
        x = s.reshape(tk // SUBLANES, SUBLANES, s.shape[-1])
        while x.shape[0] > 1:
            half = x.shape[0] // 2
            x = jnp.maximum(x[:half], x[half:])
        mx_ref[slot] = x[0]

    def update(i, slot):
        m = m_ref[...]
        m_new = jnp.maximum(m, jnp.max(mx_ref[slot], axis=0, keepdims=True))
        alpha = jnp.exp2(m - m_new)
        p = jnp.exp2(s_ref[slot] - m_new).astype(BF16)
        start = pl.multiple_of(i * tk, tk)
        pv = jnp.dot(vt_ref[:, pl.ds(start, tk)], p, preferred_element_type=F32)
        acc_ref[...] = alpha * acc_ref[...] + pv
        m_ref[...] = m_new

    m_ref[...] = jnp.full_like(m_ref, -jnp.inf)
    acc_ref[...] = jnp.zeros_like(acc_ref)
    scores(0, 0)

    def body(j, _):
        for t in range(ATTN_UNROLL):
            scores(j * ATTN_UNROLL + t + 1, (t + 1) % 2)
            update(j * ATTN_UNROLL + t, t % 2)
        return 0

    n_loop = (n_k - 1) // ATTN_UNROLL
    lax.fori_loop(0, n_loop, body, 0)
    for i in range(n_loop * ATTN_UNROLL, n_k):
        if i + 1 < n_k:
            scores(i + 1, (i + 1) % 2)
        update(i, i % 2)
    acc = acc_ref[...]
    o_ref[0] = (acc[:V_HEAD] / acc[V_HEAD:V_HEAD + 1]).T.astype(o_ref.dtype)


def _attention(qt, k, vt, batch, seq_len):
    k = k.reshape(batch, seq_len, N_HEADS * HEAD_PAD)
    tq = _tile(seq_len, 2048)
    tk = _tile(seq_len, 1024)
    out = pl.pallas_call(
        functools.partial(_flash_kernel, tk=tk),
        grid=(batch, N_HEADS, seq_len // tq),
        in_specs=[
            pl.BlockSpec((HEAD_PAD, tq), lambda b, h, i: (h, b * (seq_len // tq) + i)),
            pl.BlockSpec((1, seq_len, HEAD_PAD), lambda b, h, i: (b, 0, h)),
            pl.BlockSpec((ACC_ROWS, seq_len), lambda b, h, i: (h, b)),
        ],
        out_specs=pl.BlockSpec((1, tq, V_HEAD), lambda b, h, i: (b, i, h)),
        out_shape=jax.ShapeDtypeStruct((batch, seq_len, N_HEADS * V_HEAD), BF16),
        scratch_shapes=[pltpu.VMEM((2, tk, tq), F32), pltpu.VMEM((2, SUBLANES, tq), F32),
                        pltpu.VMEM((1, tq), F32), pltpu.VMEM((ACC_ROWS, tq), F32)],
        compiler_params=_params(("arbitrary", "arbitrary", "arbitrary")),
        name="attention",
    )(qt, k, vt)
    return out.reshape(batch * seq_len, N_HEADS * V_HEAD)


def _ssm_matrices(lam_re, lam_im, log_dt, b_re, b_im, c_re, c_im):
    t_c = SSM_CHUNK
    n_g = lam_re.shape[1]
    w_in, w_state, a_r, a_s = [], [], [], []
    toep = 0.0
    for d in range(2):
        dt = jnp.exp(log_dt[d].astype(F32))[:, None]
        lam = lax.complex(lam_re[d].astype(F32), lam_im[d].astype(F32))
        a = jnp.exp(lam * dt)
        coef = (a - 1.0) / lam
        bt = lax.complex(b_re[d].astype(F32), b_im[d].astype(F32)) * coef[:, :, None]
        cc = lax.complex(c_re[d].astype(F32), c_im[d].astype(F32))
        steps = jnp.arange(t_c + 1, dtype=F32)
        apow = jnp.exp(lam[None] * dt[None] * steps[:, None, None])
        in_pow = apow[:t_c][::-1] if d == 0 else apow[:t_c]
        win = jnp.einsum("sgp,gpc->gscp", in_pow, bt, precision=lax.Precision.HIGHEST).reshape(n_g, LANES, SSM_STATE)
        w_in.append(jnp.concatenate([win.real, win.imag, win.imag, win.real], axis=-1))
        out_pow = apow[1:] if d == 0 else apow[1:][::-1]
        wst = jnp.einsum("gcp,tgp->gptc", cc, out_pow, precision=lax.Precision.HIGHEST).reshape(n_g, SSM_STATE, LANES)
        w_state.append(jnp.concatenate([wst.real, -wst.imag], axis=1))
        kern = jnp.einsum("gcp,kgp,gpe->kgce", cc, apow[:t_c], bt,
                          precision=lax.Precision.HIGHEST).real
        lag = np.arange(t_c)[None, :] - np.arange(t_c)[:, None]
        lag = lag if d == 0 else -lag
        sel = (lag[:, :, None] == np.arange(t_c)[None, None, :]).astype(np.float32)
        toep = toep + jnp.einsum("stk,kgce->gsetc", sel, kern,
                                 precision=lax.Precision.HIGHEST).reshape(n_g, LANES, LANES)
        a_t = apow[t_c]
        a_r.append(jnp.concatenate([a_t.real, a_t.real], axis=-1))
        a_s.append(jnp.concatenate([-a_t.imag, a_t.imag], axis=-1))
    w_out = jnp.concatenate([toep, w_state[0], w_state[1]], axis=1)
    return dict(w_in_f=w_in[0].astype(BF16), w_in_b=w_in[1].astype(BF16), w_out=w_out.astype(BF16),
                ar_f=a_r[0], as_f=a_s[0], ar_b=a_r[1], as_b=a_s[1])


def _ssm_state_kernel(uf_ref, ub_ref, wf_ref, wb_ref, arf_ref, asf_ref, arb_ref, asb_ref,
                      xf_ref, xb_ref, vf, vft, vb, vbt, carry, *, tj, pitch):
    n_g = uf_ref.shape[0]
    n_blk = n_g // SUBLANES

    @pl.when(pl.program_id(1) == 0)
    def _():
        carry[...] = jnp.zeros_like(carry)

    def project(blk, _):
        for gl in range(SUBLANES):
            g = blk * SUBLANES + gl
            row = pl.multiple_of(g * pitch, SUBLANES)
            pf = jnp.dot(uf_ref[g], wf_ref[g], preferred_element_type=F32)
            vf[pl.ds(row, tj), :] = pf[:, :LANES]
            vft[pl.ds(row, tj), :] = pf[:, LANES:]
            pb = jnp.dot(ub_ref[g], wb_ref[g], preferred_element_type=F32)
            vb[pl.ds(row, tj), :] = pb[:, :LANES]
            vbt[pl.ds(row, tj), :] = pb[:, LANES:]
        return 0

    lax.fori_loop(0, n_blk, project, 0)

    def scan(j, state):
        jb = tj - 1 - j
        new = []
        for blk in range(n_blk):
            base = blk * SUBLANES * pitch
            gs = pl.ds(blk * SUBLANES, SUBLANES)
            for d, (v, vt, ar_ref, as_ref, jj) in enumerate(
                    ((vf, vft, arf_ref, asf_ref, j), (vb, vbt, arb_ref, asb_ref, jb))):
                x, xt = state[(blk * 2 + d) * 2], state[(blk * 2 + d) * 2 + 1]
                rows = pl.ds(base + jj, SUBLANES, stride=pitch)
                vx = v[rows, :]
                vxt = vt[rows, :]
                v[rows, :] = x
                ar = ar_ref[gs, :]
                a_s = as_ref[gs, :]
                new.append(ar * x + a_s * xt + vx)
                new.append(ar * xt - a_s * x + vxt)
        return tuple(new)

    init = tuple(carry[i] for i in range(4 * n_blk))
    final = lax.fori_loop(0, tj, scan, init)
    for i in range(4 * n_blk):
        carry[i] = final[i]

    def emit(g, _):
        row = pl.multiple_of(g * pitch, SUBLANES)
        xf_ref[g] = vf[pl.ds(row, tj), :].astype(BF16)
        xb_ref[g] = vb[pl.ds(row, tj), :].astype(BF16)
        return 0

    lax.fori_loop(0, n_g, emit, 0, unroll=4)


def _ssm_out_kernel(u_ref, xf_ref, xb_ref, w_ref, y_ref, *, tj):
    n_blk = y_ref.shape[0]

    def body(blk, _):
        groups = []
        for gl in range(SUBLANES):
            g = blk * SUBLANES + gl
            lhs = jnp.concatenate([u_ref[g], xf_ref[g], xb_ref[g]], axis=1)
            groups.append(jnp.dot(lhs, w_ref[g], preferred_element_type=F32))
        steps = _block_transpose(groups)
        for t in range(SSM_CHUNK):
            y_ref[blk, pl.ds(t, tj, stride=SSM_CHUNK), :] = steps[t]
        return 0

    lax.fori_loop(0, n_blk, body, 0)


def _ssm(uc, batch, seq_len, w):
    n_g = uc.shape[0]
    n_j = seq_len // SSM_CHUNK
    tj = _tile(n_j, 128)
    n_t = n_j // tj
    pitch = tj + SSM_PITCH_PAD
    blk = (n_g, tj, LANES)
    fwd = lambda b, i: (0, b * n_t + i, 0)
    bwd = lambda b, i: (0, b * n_t + n_t - 1 - i, 0)
    scan_buf = pltpu.VMEM((n_g * pitch, LANES), F32)
    xf, xb = pl.pallas_call(
        functools.partial(_ssm_state_kernel, tj=tj, pitch=pitch),
        grid=(batch, n_t),
        in_specs=[
            pl.BlockSpec(blk, fwd),
            pl.BlockSpec(blk, bwd),
            _const_spec(w["w_in_f"].shape),
            _const_spec(w["w_in_b"].shape),
            _const_spec(w["ar_f"].shape),
            _const_spec(w["as_f"].shape),
            _const_spec(w["ar_b"].shape),
            _const_spec(w["as_b"].shape),
        ],
        out_specs=[pl.BlockSpec(blk, fwd), pl.BlockSpec(blk, bwd)],
        out_shape=[jax.ShapeDtypeStruct(uc.shape, BF16)] * 2,
        scratch_shapes=[scan_buf, scan_buf, scan_buf, scan_buf,
                        pltpu.VMEM((4 * (n_g // SUBLANES), SUBLANES, LANES), F32)],
        compiler_params=_params(("arbitrary", "arbitrary")),
        name="ssm_state",
    )(uc, uc, w["w_in_f"], w["w_in_b"], w["ar_f"], w["as_f"], w["ar_b"], w["as_b"])
    n_blk = n_g // SUBLANES
    return pl.pallas_call(
        functools.partial(_ssm_out_kernel, tj=tj),
        grid=(batch, n_t),
        in_specs=[pl.BlockSpec(blk, fwd), pl.BlockSpec(blk, fwd), pl.BlockSpec(blk, fwd),
                  _const_spec(w["w_out"].shape)],
        out_specs=pl.BlockSpec((n_blk, tj * SSM_CHUNK, LANES), fwd),
        out_shape=jax.ShapeDtypeStruct((n_blk, batch * seq_len, LANES), F32),
        compiler_params=_params(("arbitrary", "arbitrary")),
        name="ssm_out",
    )(uc, xf, xb, w["w_out"])


def _mix_out_kernel(ys_ref, u_ref, a_ref, x_ref, d_ref, wglu_ref, bglu_ref, ga_ref, gs_ref,
                    woa_ref, wos_ref, gpost_ref, gffn_ref, x1_ref, h2_ref):
    n_blk = ys_ref.shape[0]
    ys = jnp.concatenate([ys_ref[b] for b in range(n_blk)], axis=1)
    u = jnp.concatenate([u_ref[b] for b in range(n_blk)], axis=1)
    y = jax.nn.gelu(ys + d_ref[...] * u)
    gate = jax.nn.sigmoid(jnp.dot(y.astype(BF16), wglu_ref[...], preferred_element_type=F32)
                          + bglu_ref[...])
    s = y * gate
    an = _rms(a_ref[...].astype(F32), ga_ref[...]).astype(BF16)
    sn = _rms(s, gs_ref[...]).astype(BF16)
    m = (jnp.dot(an, woa_ref[...], preferred_element_type=F32)
         + jnp.dot(sn, wos_ref[...], preferred_element_type=F32))
    x1 = x_ref[...] + _rms(m, gpost_ref[...])
    x1_ref[...] = x1
    h2_ref[...] = _rms(x1, gffn_ref[...]).astype(BF16)


def _mix_out(ys, u, a, x2d, w):
    t, d = x2d.shape
    n_a = a.shape[1]
    n_blk = ys.shape[0]
    n_s = n_blk * LANES
    tm = _tile(t, 512)
    row = lambda i: (i, 0)
    return pl.pallas_call(
        _mix_out_kernel,
        grid=(t // tm,),
        in_specs=[
            pl.BlockSpec((n_blk, tm, LANES), lambda i: (0, i, 0)),
            pl.BlockSpec((n_blk, tm, LANES), lambda i: (0, i, 0)),
            pl.BlockSpec((tm, n_a), row),
            pl.BlockSpec((tm, d), row),
            _const_spec((1, n_s)),
            _const_spec(w["w_glu"].shape),
            _const_spec((1, n_s)),
            _const_spec((1, n_a)),
            _const_spec((1, n_s)),
            _const_spec(w["w_out_a"].shape),
            _const_spec(w["w_out_s"].shape),
            _const_spec((1, d)),
            _const_spec((1, d)),
        ],
        out_specs=[pl.BlockSpec((tm, d), row), pl.BlockSpec((tm, d), row)],
        out_shape=[jax.ShapeDtypeStruct((t, d), F32), jax.ShapeDtypeStruct((t, d), BF16)],
        compiler_params=_params(("arbitrary",)),
        name="mix_out",
    )(ys, u, a, x2d, w["ssm_d"], w["w_glu"], w["b_glu"], w["g_out_attn"], w["g_out_ssm"],
      w["w_out_a"], w["w_out_s"], w["g_post_mix"], w["g_pre_ffn"])


def _ffn_kernel(h_ref, x1_ref, wg_ref, wu_ref, wd_ref, gpost_ref, o_ref, acc_ref):
    j = pl.program_id(1)

    @pl.when(j == 0)
    def _():
        acc_ref[...] = jnp.zeros_like(acc_ref)

    h = h_ref[...]
    gate = jnp.dot(h, wg_ref[...], preferred_element_type=F32)
    up = jnp.dot(h, wu_ref[...], preferred_element_type=F32)
    act = (jax.nn.silu(gate) * up).astype(BF16)
    acc_ref[...] += jnp.dot(act, wd_ref[...], preferred_element_type=F32)

    @pl.when(j == pl.num_programs(1) - 1)
    def _():
        o_ref[...] = x1_ref[...] + _rms(acc_ref[...], gpost_ref[...])


def _ffn(h2, x1, w):
    t, d = x1.shape
    d_ff = w["w_gate"].shape[1]
    tm = _tile(t, 512)
    tf = _tile(d_ff, 512)
    return pl.pallas_call(
        _ffn_kernel,
        grid=(t // tm, d_ff // tf),
        in_specs=[
            pl.BlockSpec((tm, d), lambda i, j: (i, 0)),
            pl.BlockSpec((tm, d), lambda i, j: (i, 0)),
            pl.BlockSpec((d, tf), lambda i, j: (0, j)),
            pl.BlockSpec((d, tf), lambda i, j: (0, j)),
            pl.BlockSpec((tf, d), lambda i, j: (j, 0)),
            pl.BlockSpec((1, d), lambda i, j: (0, 0)),
        ],
        out_specs=pl.BlockSpec((tm, d), lambda i, j: (i, 0)),
        out_shape=jax.ShapeDtypeStruct((t, d), F32),
        scratch_shapes=[pltpu.VMEM((tm, d), F32)],
        compiler_params=_params(("arbitrary", "arbitrary")),
        name="ffn",
    )(h2, x1, w["w_gate"], w["w_up"], w["w_down"], w["g_post_ffn"])


def _rope_table(length):
    assert length % LANES == 0
    inv = 1.0 / (ROPE_THETA ** (jnp.arange(0, QK_ROPE, 2, dtype=F32) / QK_ROPE))
    hi = (jnp.arange(length // LANES, dtype=F32) * LANES)[:, None] * inv[None, :]
    lo = jnp.arange(LANES, dtype=F32)[:, None] * inv[None, :]
    ch, sh = jnp.cos(hi)[:, None, :], jnp.sin(hi)[:, None, :]
    cl, sl = jnp.cos(lo)[None, :, :], jnp.sin(lo)[None, :, :]
    cos = (ch * cl - sh * sl).reshape(length, -1)
    sin = (sh * cl + ch * sl).reshape(length, -1)
    return jnp.concatenate([cos, cos, -sin, sin], axis=-1)


def _swap_halves(wr):
    half = QK_ROPE // 2
    return jnp.concatenate([wr[..., half:], wr[..., :half]], axis=-1)


def _prepare_weights(g_pre_mix, w_in, g_q_a, w_q_b, g_kv_a, w_kv_b, lam_re, lam_im, log_dt,
                     b_re, b_im, c_re, c_im, ssm_d, w_glu, b_glu, g_out_attn, g_out_ssm, w_out,
                     g_post_mix, g_pre_ffn, w_gate, w_up, w_down, g_post_ffn):
    row = lambda g: g.astype(F32).reshape(1, -1)
    i1 = Q_LORA + KV_LORA
    i2 = i1 + QK_ROPE
    w_rope = w_in[:, i1:i2]
    w_in_ext = jnp.concatenate([w_in[:, :i1], w_in[:, i2:], w_rope, _swap_halves(w_rope)], axis=1)
    wq = w_q_b.reshape(Q_LORA, N_HEADS, QK_NOPE + QK_ROPE)
    wq_ext = jnp.concatenate([wq, _swap_halves(wq[..., QK_NOPE:])], axis=-1)
    wkv = w_kv_b.reshape(KV_LORA, N_HEADS, QK_NOPE + V_HEAD)
    n_a = N_HEADS * V_HEAD
    out = dict(
        n_u=w_in.shape[1] - i2,
        q_scale=float((QK_NOPE + QK_ROPE) ** -0.5 * math.log2(math.e)),
        g_pre_mix=row(g_pre_mix), w_in=w_in_ext.astype(BF16),
        g_q_a=row(g_q_a), w_qt=wq_ext.reshape(Q_LORA, N_HEADS * HEAD_PAD).T.astype(BF16),
        g_kv_a=row(g_kv_a),
        w_k=wkv[..., :QK_NOPE].reshape(KV_LORA, N_HEADS * QK_NOPE).astype(BF16),
        w_vt=wkv[..., QK_NOPE:].reshape(KV_LORA, N_HEADS * V_HEAD).T.astype(BF16),
        ssm_d=row(ssm_d), w_glu=w_glu.astype(BF16), b_glu=row(b_glu),
        g_out_attn=row(g_out_attn), g_out_ssm=row(g_out_ssm),
        w_out_a=w_out[:n_a].astype(BF16), w_out_s=w_out[n_a:].astype(BF16),
        g_post_mix=row(g_post_mix), g_pre_ffn=row(g_pre_ffn),
        w_gate=w_gate.astype(BF16), w_up=w_up.astype(BF16), w_down=w_down.astype(BF16),
        g_post_ffn=row(g_post_ffn),
    )
    out.update(_ssm_matrices(lam_re, lam_im, log_dt, b_re, b_im, c_re, c_im))
    return out


def _layer(x, w):
    batch, seq_len, d = x.shape
    x2d = x.reshape(batch * seq_len, d)
    cs = _rope_table(seq_len)
    qt, k, vt, u, uc = _in_proj(x2d, seq_len, w, cs, cs.T)
    a = _attention(qt, k, vt, batch, seq_len)
    ys = _ssm(uc, batch, seq_len, w)
    x1, h2 = _mix_out(ys, u, a, x2d, w)
    return _ffn(h2, x1, w).reshape(batch, seq_len, d)


def kernel(x_prompt, x_sample, g_pre_mix, w_in, g_q_a, w_q_b, g_kv_a, w_kv_b, ssm_lam_re, ssm_lam_im, ssm_log_dt, ssm_b_re, ssm_b_im, ssm_c_re, ssm_c_im, ssm_d, w_glu, b_glu, g_out_attn, g_out_ssm, w_out, g_post_mix, g_pre_ffn, w_gate, w_up, w_down, g_post_ffn):
    weights = (g_pre_mix, w_in, g_q_a, w_q_b, g_kv_a, w_kv_b, ssm_lam_re, ssm_lam_im, ssm_log_dt,
               ssm_b_re, ssm_b_im, ssm_c_re, ssm_c_im, ssm_d, w_glu, b_glu, g_out_attn, g_out_ssm,
               w_out, g_post_mix, g_pre_ffn, w_gate, w_up, w_down, g_post_ffn)
    depth = g_pre_mix.shape[0]
    for layer in range(depth):
        w = _prepare_weights(*[p[layer] for p in weights])
        x_prompt = _layer(x_prompt, w)
        x_sample = _layer(x_sample, w)
    return (x_prompt, x_sample)
```

```python
import functools
import math

import jax
import jax.numpy as jnp
import numpy as np
from jax import lax
from jax.experimental import pallas as pl
from jax.experimental.pallas import tpu as pltpu

F32 = jnp.float32
BF16 = jnp.bfloat16

RMS_EPS = 1e-6
ROPE_THETA = 10000.0
LANES = 128
SUBLANES = 8
VMEM_LIMIT_BYTES = 60 * 1024 * 1024

N_HEADS = 8
QK_NOPE = 128
QK_ROPE = 64
V_HEAD = 128
HEAD_PAD = 256
Q_LORA = 512
KV_LORA = 256
ACC_ROWS = V_HEAD + 16
ATTN_UNROLL = 2
ATTN_LONG_SEQ = 8192
SSM_GROUP = 16
SSM_STATE = 64
SSM_CHUNK = LANES // SSM_GROUP
SSM_PITCH_PAD = 8


def _rms(x, g):
    return x * lax.rsqrt(jnp.mean(x * x, axis=-1, keepdims=True) + RMS_EPS) * g


def _params(sem):
    return pltpu.CompilerParams(dimension_semantics=sem, vmem_limit_bytes=VMEM_LIMIT_BYTES)


def _tile(n, target):
    t = min(n, target)
    while n % t:
        t //= 2
    return t


def _block_transpose(a):
    lane = lax.broadcasted_iota(jnp.int32, a[0].shape, 1)
    for d in (4, 2, 1):
        shift = SSM_GROUP * d
        upper = (lane & shift) != 0
        new = list(a)
        for i in range(len(a)):
            if i & d:
                continue
            x, y = a[i], a[i | d]
            new[i] = jnp.where(upper, pltpu.roll(y, shift, 1), x)
            new[i | d] = jnp.where(upper, y, pltpu.roll(x, LANES - shift, 1))
        a = new
    return a


def _const_spec(shape):
    nd = len(shape)
    return pl.BlockSpec(shape, lambda *_: (0,) * nd, pipeline_mode=pl.Buffered(1))


def _in_proj_kernel(x_ref, gpre_ref, win_ref, gq_ref, wqt_ref, gkv_ref, wk_ref, wvt_ref, cs_ref, cst_ref,
                    qt_ref, k_ref, vt_ref, u_ref, uc_ref, *, q_scale):
    x = x_ref[...]
    h = _rms(x, gpre_ref[...])
    z = jnp.dot(h.astype(BF16), win_ref[...], preferred_element_type=F32)
    c_q = z[:, :Q_LORA]
    c_kv = z[:, Q_LORA:Q_LORA + KV_LORA]
    n_blk = u_ref.shape[0]
    u0 = Q_LORA + KV_LORA
    for blk in range(n_blk):
        u_ref[blk] = z[:, u0 + blk * LANES:u0 + (blk + 1) * LANES]
    k_rope = z[:, u0 + n_blk * LANES:]
    cs = cs_ref[...]

    def rotate(t):
        t = t * cs
        return t + pltpu.roll(t, QK_ROPE, 1)

    cqn = _rms(c_q, gq_ref[...]).astype(BF16)
    qt = lax.dot_general(wqt_ref[...], cqn, (((1,), (1,)), ((), ())),
                         preferred_element_type=F32) * q_scale
    cst = cst_ref[...]
    for hd in range(N_HEADS):
        lo = hd * HEAD_PAD
        qt_ref[lo:lo + QK_NOPE] = qt[lo:lo + QK_NOPE].astype(BF16)
        t = qt[lo + QK_NOPE:lo + HEAD_PAD] * cst
        roped = (t[:QK_ROPE] + t[QK_ROPE:]).astype(BF16)
        qt_ref[lo + QK_NOPE:lo + QK_NOPE + QK_ROPE] = roped
        qt_ref[lo + QK_NOPE + QK_ROPE:lo + HEAD_PAD] = roped

    ckvn = _rms(c_kv, gkv_ref[...]).astype(BF16)
    k_nope = jnp.dot(ckvn, wk_ref[...], preferred_element_type=F32)
    vt = lax.dot_general(wvt_ref[...], ckvn, (((1,), (1,)), ((), ())),
                         preferred_element_type=F32).astype(BF16)
    ones = jnp.ones((ACC_ROWS - V_HEAD, vt.shape[1]), BF16)
    for hd in range(N_HEADS):
        vt_ref[hd * ACC_ROWS:hd * ACC_ROWS + V_HEAD] = vt[hd * V_HEAD:(hd + 1) * V_HEAD]
        vt_ref[hd * ACC_ROWS + V_HEAD:(hd + 1) * ACC_ROWS] = ones
    lane = lax.broadcasted_iota(jnp.int32, k_rope.shape, 1)
    kr = jnp.where(lane < QK_ROPE, rotate(k_rope), 0.0).astype(BF16)
    for hd in range(N_HEADS):
        lo = hd * HEAD_PAD
        k_ref[:, lo:lo + QK_NOPE] = k_nope[:, hd * QK_NOPE:(hd + 1) * QK_NOPE].astype(BF16)
        k_ref[:, lo + QK_NOPE:lo + HEAD_PAD] = kr

    n_j = x.shape[0] // SSM_CHUNK
    for blk in range(n_blk):
        steps = [u_ref[blk, pl.ds(t, n_j, stride=SSM_CHUNK), :] for t in range(SSM_CHUNK)]
        groups = _block_transpose(steps)
        for gl in range(SUBLANES):
            uc_ref[blk * SUBLANES + gl] = groups[gl].astype(BF16)


def _in_proj(x2d, seq_len, w, cs, cst):
    t, d = x2d.shape
    tm = _tile(seq_len, 512)
    n_blk = w["n_u"] // LANES
    n_g = w["n_u"] // SSM_GROUP
    kern = functools.partial(_in_proj_kernel, q_scale=w["q_scale"])
    tiles_per_seq = seq_len // tm
    row = lambda i: (i, 0)
    return pl.pallas_call(
        kern,
        grid=(t // tm,),
        in_specs=[
            pl.BlockSpec((tm, d), row),
            _const_spec((1, d)),
            _const_spec(w["w_in"].shape),
            _const_spec((1, Q_LORA)),
            _const_spec(w["w_qt"].shape),
            _const_spec((1, KV_LORA)),
            _const_spec(w["w_k"].shape),
            _const_spec(w["w_vt"].shape),
            pl.BlockSpec((tm, LANES), lambda i: (i % tiles_per_seq, 0)),
            pl.BlockSpec((LANES, tm), lambda i: (0, i % tiles_per_seq)),
        ],
        out_specs=[
            pl.BlockSpec((N_HEADS * HEAD_PAD, tm), lambda i: (0, i)),
            pl.BlockSpec((tm, N_HEADS * HEAD_PAD), row),
            pl.BlockSpec((N_HEADS * ACC_ROWS, tm), lambda i: (0, i)),
            pl.BlockSpec((n_blk, tm, LANES), lambda i: (0, i, 0)),
            pl.BlockSpec((n_g, tm // SSM_CHUNK, LANES), lambda i: (0, i, 0)),
        ],
        out_shape=[
            jax.ShapeDtypeStruct((N_HEADS * HEAD_PAD, t), BF16),
            jax.ShapeDtypeStruct((t, N_HEADS * HEAD_PAD), BF16),
            jax.ShapeDtypeStruct((N_HEADS * ACC_ROWS, t), BF16),
            jax.ShapeDtypeStruct((n_blk, t, LANES), F32),
            jax.ShapeDtypeStruct((n_g, t // SSM_CHUNK, LANES), BF16),
        ],
        compiler_params=_params(("arbitrary",)),
        name="in_proj",
    )(x2d, w["g_pre_mix"], w["w_in"], w["g_q_a"], w["w_qt"], w["g_kv_a"], w["w_k"], w["w_vt"], cs, cst)


def _flash_kernel(qt_ref, k_ref, vt_ref, o_ref, s_ref, mx_ref, m_ref, acc_ref, *, tk):
    n_k = k_ref.shape[1] // tk

    def scores(i, slot):
        start = pl.multiple_of(i * tk, tk)
        s = jnp.dot(k_ref[0, pl.ds(start, tk), :], qt_ref[...],
                    preferred_element_type=F32)
        s_ref[slot] = s
        x = s.reshape(tk // SUBLANES, SUBLANES, s.shape[-1])
        while x.shape[0] > 1:
            half = x.shape[0] // 2
            x = jnp.maximum(x[:half], x[half:])
        mx_ref[slot] = x[0]

    def update(i, slot):
        m = m_ref[...]
        m_new = jnp.maximum(m, jnp.max(mx_ref[slot], axis=0, keepdims=True))
        alpha = jnp.exp2(m - m_new)
        p = jnp.exp2(s_ref[slot] - m_new).astype(BF16)
        start = pl.multiple_of(i * tk, tk)
        pv = jnp.dot(vt_ref[:, pl.ds(start, tk)], p, preferred_element_type=F32)
        acc_ref[...] = alpha * acc_ref[...] + pv
        m_ref[...] = m_new

    m_ref[...] = jnp.full_like(m_ref, -jnp.inf)
    acc_ref[...] = jnp.zeros_like(acc_ref)
    scores(0, 0)

    def body(j, _):
        for t in range(ATTN_UNROLL):
            scores(j * ATTN_UNROLL + t + 1, (t + 1) % 2)
            update(j * ATTN_UNROLL + t, t % 2)
        return 0

    n_loop = (n_k - 1) // ATTN_UNROLL
    lax.fori_loop(0, n_loop, body, 0)
    for i in range(n_loop * ATTN_UNROLL, n_k):
        if i + 1 < n_k:
            scores(i + 1, (i + 1) % 2)
        update(i, i % 2)
    acc = acc_ref[...]
    o_ref[0] = (acc[:V_HEAD] / acc[V_HEAD:V_HEAD + 1]).T.astype(o_ref.dtype)


def _attention(qt, k, vt, batch, seq_len):
    k = k.reshape(batch, seq_len, N_HEADS * HEAD_PAD)
    tq = _tile(seq_len, 2048)
    long_seq = seq_len >= ATTN_LONG_SEQ
    tk = _tile(seq_len, 2048 if long_seq else 1024)
    kv_mode = dict(pipeline_mode=pl.Buffered(1)) if long_seq else {}
    out = pl.pallas_call(
        functools.partial(_flash_kernel, tk=tk),
        grid=(batch, N_HEADS, seq_len // tq),
        in_specs=[
            pl.BlockSpec((HEAD_PAD, tq), lambda b, h, i: (h, b * (seq_len // tq) + i)),
            pl.BlockSpec((1, seq_len, HEAD_PAD), lambda b, h, i: (b, 0, h), **kv_mode),
            pl.BlockSpec((ACC_ROWS, seq_len), lambda b, h, i: (h, b), **kv_mode),
        ],
        out_specs=pl.BlockSpec((1, tq, V_HEAD), lambda b, h, i: (b, i, h)),
        out_shape=jax.ShapeDtypeStruct((batch, seq_len, N_HEADS * V_HEAD), BF16),
        scratch_shapes=[pltpu.VMEM((2, tk, tq), F32), pltpu.VMEM((2, SUBLANES, tq), F32),
                        pltpu.VMEM((1, tq), F32), pltpu.VMEM((ACC_ROWS, tq), F32)],
        compiler_params=_params(("arbitrary", "arbitrary", "arbitrary")),
        name="attention",
    )(qt, k, vt)
    return out.reshape(batch * seq_len, N_HEADS * V_HEAD)


def _ssm_matrices(lam_re, lam_im, log_dt, b_re, b_im, c_re, c_im):
    t_c = SSM_CHUNK
    n_g = lam_re.shape[1]
    w_in, w_state, a_r, a_s = [], [], [], []
    toep = 0.0
    for d in range(2):
        dt = jnp.exp(log_dt[d].astype(F32))[:, None]
        lam = lax.complex(lam_re[d].astype(F32), lam_im[d].astype(F32))
        a = jnp.exp(lam * dt)
        coef = (a - 1.0) / lam
        bt = lax.complex(b_re[d].astype(F32), b_im[d].astype(F32)) * coef[:, :, None]
        cc = lax.complex(c_re[d].astype(F32), c_im[d].astype(F32))
        steps = jnp.arange(t_c + 1, dtype=F32)
        apow = jnp.exp(lam[None] * dt[None] * steps[:, None, None])
        in_pow = apow[:t_c][::-1] if d == 0 else apow[:t_c]
        win = jnp.einsum("sgp,gpc->gscp", in_pow, bt, precision=lax.Precision.HIGHEST).reshape(n_g, LANES, SSM_STATE)
        w_in.append(jnp.concatenate([win.real, win.imag, win.imag, win.real], axis=-1))
        out_pow = apow[1:] if d == 0 else apow[1:][::-1]
        wst = jnp.einsum("gcp,tgp->gptc", cc, out_pow, precision=lax.Precision.HIGHEST).reshape(n_g, SSM_STATE, LANES)
        w_state.append(jnp.concatenate([wst.real, -wst.imag], axis=1))
        kern = jnp.einsum("gcp,kgp,gpe->kgce", cc, apow[:t_c], bt,
                          precision=lax.Precision.HIGHEST).real
        lag = np.arange(t_c)[None, :] - np.arange(t_c)[:, None]
        lag = lag if d == 0 else -lag
        sel = (lag[:, :, None] == np.arange(t_c)[None, None, :]).astype(np.float32)
        toep = toep + jnp.einsum("stk,kgce->gsetc", sel, kern,
                                 precision=lax.Precision.HIGHEST).reshape(n_g, LANES, LANES)
        a_t = apow[t_c]
        a_r.append(jnp.concatenate([a_t.real, a_t.real], axis=-1))
        a_s.append(jnp.concatenate([-a_t.imag, a_t.imag], axis=-1))
    w_out = jnp.concatenate([toep, w_state[0], w_state[1]], axis=1)
    return dict(w_in_f=w_in[0].astype(BF16), w_in_b=w_in[1].astype(BF16), w_out=w_out.astype(BF16),
                ar_f=a_r[0], as_f=a_s[0], ar_b=a_r[1], as_b=a_s[1])


def _ssm_state_kernel(uf_ref, ub_ref, wf_ref, wb_ref, arf_ref, asf_ref, arb_ref, asb_ref,
                      xf_ref, xb_ref, vf, vft, vb, vbt, carry, *, tj, pitch):
    n_g = uf_ref.shape[0]
    n_blk = n_g // SUBLANES

    @pl.when(pl.program_id(1) == 0)
    def _():
        carry[...] = jnp.zeros_like(carry)

    def project(blk, _):
        for gl in range(SUBLANES):
            g = blk * SUBLANES + gl
            row = pl.multiple_of(g * pitch, SUBLANES)
            pf = jnp.dot(uf_ref[g], wf_ref[g], preferred_element_type=F32)
            vf[pl.ds(row, tj), :] = pf[:, :LANES]
            vft[pl.ds(row, tj), :] = pf[:, LANES:]
            pb = jnp.dot(ub_ref[g], wb_ref[g], preferred_element_type=F32)
            vb[pl.ds(row, tj), :] = pb[:, :LANES]
            vbt[pl.ds(row, tj), :] = pb[:, LANES:]
        return 0

    lax.fori_loop(0, n_blk, project, 0)

    def scan(j, state):
        jb = tj - 1 - j
        new = []
        for blk in range(n_blk):
            base = blk * SUBLANES * pitch
            gs = pl.ds(blk * SUBLANES, SUBLANES)
            for d, (v, vt, ar_ref, as_ref, jj) in enumerate(
                    ((vf, vft, arf_ref, asf_ref, j), (vb, vbt, arb_ref, asb_ref, jb))):
                x, xt = state[(blk * 2 + d) * 2], state[(blk * 2 + d) * 2 + 1]
                rows = pl.ds(base + jj, SUBLANES, stride=pitch)
                vx = v[rows, :]
                vxt = vt[rows, :]
                v[rows, :] = x
                ar = ar_ref[gs, :]
                a_s = as_ref[gs, :]
                new.append(ar * x + a_s * xt + vx)
                new.append(ar * xt - a_s * x + vxt)
        return tuple(new)

    init = tuple(carry[i] for i in range(4 * n_blk))
    final = lax.fori_loop(0, tj, scan, init)
    for i in range(4 * n_blk):
        carry[i] = final[i]

    def emit(g, _):
        row = pl.multiple_of(g * pitch, SUBLANES)
        xf_ref[g] = vf[pl.ds(row, tj), :].astype(BF16)
        xb_ref[g] = vb[pl.ds(row, tj), :].astype(BF16)
        return 0

    lax.fori_loop(0, n_g, emit, 0, unroll=4)


def _ssm_out_kernel(u_ref, xf_ref, xb_ref, w_ref, y_ref, *, tj):
    n_blk = y_ref.shape[0]

    def body(blk, _):
        groups = []
        for gl in range(SUBLANES):
            g = blk * SUBLANES + gl
            lhs = jnp.concatenate([u_ref[g], xf_ref[g], xb_ref[g]], axis=1)
            groups.append(jnp.dot(lhs, w_ref[g], preferred_element_type=F32))
        steps = _block_transpose(groups)
        for t in range(SSM_CHUNK):
            y_ref[blk, pl.ds(t, tj, stride=SSM_CHUNK), :] = steps[t]
        return 0

    lax.fori_loop(0, n_blk, body, 0)


def _ssm(uc, batch, seq_len, w):
    n_g = uc.shape[0]
    n_j = seq_len // SSM_CHUNK
    tj = _tile(n_j, 128)
    n_t = n_j // tj
    pitch = tj + SSM_PITCH_PAD
    blk = (n_g, tj, LANES)
    fwd = lambda b, i: (0, b * n_t + i, 0)
    bwd = lambda b, i: (0, b * n_t + n_t - 1 - i, 0)
    scan_buf = pltpu.VMEM((n_g * pitch, LANES), F32)
    xf, xb = pl.pallas_call(
        functools.partial(_ssm_state_kernel, tj=tj, pitch=pitch),
        grid=(batch, n_t),
        in_specs=[
            pl.BlockSpec(blk, fwd),
            pl.BlockSpec(blk, bwd),
            _const_spec(w["w_in_f"].shape),
            _const_spec(w["w_in_b"].shape),
            _const_spec(w["ar_f"].shape),
            _const_spec(w["as_f"].shape),
            _const_spec(w["ar_b"].shape),
            _const_spec(w["as_b"].shape),
        ],
        out_specs=[pl.BlockSpec(blk, fwd), pl.BlockSpec(blk, bwd)],
        out_shape=[jax.ShapeDtypeStruct(uc.shape, BF16)] * 2,
        scratch_shapes=[scan_buf, scan_buf, scan_buf, scan_buf,
                        pltpu.VMEM((4 * (n_g // SUBLANES), SUBLANES, LANES), F32)],
        compiler_params=_params(("arbitrary", "arbitrary")),
        name="ssm_state",
    )(uc, uc, w["w_in_f"], w["w_in_b"], w["ar_f"], w["as_f"], w["ar_b"], w["as_b"])
    n_blk = n_g // SUBLANES
    return pl.pallas_call(
        functools.partial(_ssm_out_kernel, tj=tj),
        grid=(batch, n_t),
        in_specs=[pl.BlockSpec(blk, fwd), pl.BlockSpec(blk, fwd), pl.BlockSpec(blk, fwd),
                  _const_spec(w["w_out"].shape)],
        out_specs=pl.BlockSpec((n_blk, tj * SSM_CHUNK, LANES), fwd),
        out_shape=jax.ShapeDtypeStruct((n_blk, batch * seq_len, LANES), F32),
        compiler_params=_params(("arbitrary", "arbitrary")),
        name="ssm_out",
    )(uc, xf, xb, w["w_out"])


def _mix_out_kernel(ys_ref, u_ref, a_ref, x_ref, d_ref, wglu_ref, bglu_ref, ga_ref, gs_ref,
                    woa_ref, wos_ref, gpost_ref, gffn_ref, x1_ref, h2_ref):
    n_blk = ys_ref.shape[0]
    ys = jnp.concatenate([ys_ref[b] for b in range(n_blk)], axis=1)
    u = jnp.concatenate([u_ref[b] for b in range(n_blk)], axis=1)
    y = jax.nn.gelu(ys + d_ref[...] * u)
    gate = jax.nn.sigmoid(jnp.dot(y.astype(BF16), wglu_ref[...], preferred_element_type=F32)
                          + bglu_ref[...])
    s = y * gate
    an = _rms(a_ref[...].astype(F32), ga_ref[...]).astype(BF16)
    sn = _rms(s, gs_ref[...]).astype(BF16)
    m = (jnp.dot(an, woa_ref[...], preferred_element_type=F32)
         + jnp.dot(sn, wos_ref[...], preferred_element_type=F32))
    x1 = x_ref[...] + _rms(m, gpost_ref[...])
    x1_ref[...] = x1
    h2_ref[...] = _rms(x1, gffn_ref[...]).astype(BF16)


def _mix_out(ys, u, a, x2d, w):
    t, d = x2d.shape
    n_a = a.shape[1]
    n_blk = ys.shape[0]
    n_s = n_blk * LANES
    tm = _tile(t, 512)
    row = lambda i: (i, 0)
    return pl.pallas_call(
        _mix_out_kernel,
        grid=(t // tm,),
        in_specs=[
            pl.BlockSpec((n_blk, tm, LANES), lambda i: (0, i, 0)),
            pl.BlockSpec((n_blk, tm, LANES), lambda i: (0, i, 0)),
            pl.BlockSpec((tm, n_a), row),
            pl.BlockSpec((tm, d), row),
            _const_spec((1, n_s)),
            _const_spec(w["w_glu"].shape),
            _const_spec((1, n_s)),
            _const_spec((1, n_a)),
            _const_spec((1, n_s)),
            _const_spec(w["w_out_a"].shape),
            _const_spec(w["w_out_s"].shape),
            _const_spec((1, d)),
            _const_spec((1, d)),
        ],
        out_specs=[pl.BlockSpec((tm, d), row), pl.BlockSpec((tm, d), row)],
        out_shape=[jax.ShapeDtypeStruct((t, d), F32), jax.ShapeDtypeStruct((t, d), BF16)],
        compiler_params=_params(("arbitrary",)),
        name="mix_out",
    )(ys, u, a, x2d, w["ssm_d"], w["w_glu"], w["b_glu"], w["g_out_attn"], w["g_out_ssm"],
      w["w_out_a"], w["w_out_s"], w["g_post_mix"], w["g_pre_ffn"])


def _ffn_kernel(h_ref, x1_ref, wg_ref, wu_ref, wd_ref, gpost_ref, o_ref, acc_ref):
    j = pl.program_id(1)

    @pl.when(j == 0)
    def _():
        acc_ref[...] = jnp.zeros_like(acc_ref)

    h = h_ref[...]
    gate = jnp.dot(h, wg_ref[...], preferred_element_type=F32)
    up = jnp.dot(h, wu_ref[...], preferred_element_type=F32)
    act = (jax.nn.silu(gate) * up).astype(BF16)
    acc_ref[...] += jnp.dot(act, wd_ref[...], preferred_element_type=F32)

    @pl.when(j == pl.num_programs(1) - 1)
    def _():
        o_ref[...] = x1_ref[...] + _rms(acc_ref[...], gpost_ref[...])


def _ffn(h2, x1, w):
    t, d = x1.shape
    d_ff = w["w_gate"].shape[1]
    tm = _tile(t, 512)
    tf = _tile(d_ff, 512)
    return pl.pallas_call(
        _ffn_kernel,
        grid=(t // tm, d_ff // tf),
        in_specs=[
            pl.BlockSpec((tm, d), lambda i, j: (i, 0)),
            pl.BlockSpec((tm, d), lambda i, j: (i, 0)),
            pl.BlockSpec((d, tf), lambda i, j: (0, j)),
            pl.BlockSpec((d, tf), lambda i, j: (0, j)),
            pl.BlockSpec((tf, d), lambda i, j: (j, 0)),
            pl.BlockSpec((1, d), lambda i, j: (0, 0)),
        ],
        out_specs=pl.BlockSpec((tm, d), lambda i, j: (i, 0)),
        out_shape=jax.ShapeDtypeStruct((t, d), F32),
        scratch_shapes=[pltpu.VMEM((tm, d), F32)],
        compiler_params=_params(("arbitrary", "arbitrary")),
        name="ffn",
    )(h2, x1, w["w_gate"], w["w_up"], w["w_down"], w["g_post_ffn"])


def _rope_table(length):
    assert length % LANES == 0
    inv = 1.0 / (ROPE_THETA ** (jnp.arange(0, QK_ROPE, 2, dtype=F32) / QK_ROPE))
    hi = (jnp.arange(length // LANES, dtype=F32) * LANES)[:, None] * inv[None, :]
    lo = jnp.arange(LANES, dtype=F32)[:, None] * inv[None, :]
    ch, sh = jnp.cos(hi)[:, None, :], jnp.sin(hi)[:, None, :]
    cl, sl = jnp.cos(lo)[None, :, :], jnp.sin(lo)[None, :, :]
    cos = (ch * cl - sh * sl).reshape(length, -1)
    sin = (sh * cl + ch * sl).reshape(length, -1)
    return jnp.concatenate([cos, cos, -sin, sin], axis=-1)


def _swap_halves(wr):
    half = QK_ROPE // 2
    return jnp.concatenate([wr[..., half:], wr[..., :half]], axis=-1)


def _prepare_weights(g_pre_mix, w_in, g_q_a, w_q_b, g_kv_a, w_kv_b, lam_re, lam_im, log_dt,
                     b_re, b_im, c_re, c_im, ssm_d, w_glu, b_glu, g_out_attn, g_out_ssm, w_out,
                     g_post_mix, g_pre_ffn, w_gate, w_up, w_down, g_post_ffn):
    row = lambda g: g.astype(F32).reshape(1, -1)
    i1 = Q_LORA + KV_LORA
    i2 = i1 + QK_ROPE
    w_rope = w_in[:, i1:i2]
    w_in_ext = jnp.concatenate([w_in[:, :i1], w_in[:, i2:], w_rope, _swap_halves(w_rope)], axis=1)
    wq = w_q_b.reshape(Q_LORA, N_HEADS, QK_NOPE + QK_ROPE)
    wq_ext = jnp.concatenate([wq, _swap_halves(wq[..., QK_NOPE:])], axis=-1)
    wkv = w_kv_b.reshape(KV_LORA, N_HEADS, QK_NOPE + V_HEAD)
    n_a = N_HEADS * V_HEAD
    out = dict(
        n_u=w_in.shape[1] - i2,
        q_scale=float((QK_NOPE + QK_ROPE) ** -0.5 * math.log2(math.e)),
        g_pre_mix=row(g_pre_mix), w_in=w_in_ext.astype(BF16),
        g_q_a=row(g_q_a), w_qt=wq_ext.reshape(Q_LORA, N_HEADS * HEAD_PAD).T.astype(BF16),
        g_kv_a=row(g_kv_a),
        w_k=wkv[..., :QK_NOPE].reshape(KV_LORA, N_HEADS * QK_NOPE).astype(BF16),
        w_vt=wkv[..., QK_NOPE:].reshape(KV_LORA, N_HEADS * V_HEAD).T.astype(BF16),
        ssm_d=row(ssm_d), w_glu=w_glu.astype(BF16), b_glu=row(b_glu),
        g_out_attn=row(g_out_attn), g_out_ssm=row(g_out_ssm),
        w_out_a=w_out[:n_a].astype(BF16), w_out_s=w_out[n_a:].astype(BF16),
        g_post_mix=row(g_post_mix), g_pre_ffn=row(g_pre_ffn),
        w_gate=w_gate.astype(BF16), w_up=w_up.astype(BF16), w_down=w_down.astype(BF16),
        g_post_ffn=row(g_post_ffn),
    )
    out.update(_ssm_matrices(lam_re, lam_im, log_dt, b_re, b_im, c_re, c_im))
    return out


def _layer(x, w):
    batch, seq_len, d = x.shape
    x2d = x.reshape(batch * seq_len, d)
    cs = _rope_table(seq_len)
    qt, k, vt, u, uc = _in_proj(x2d, seq_len, w, cs, cs.T)
    a = _attention(qt, k, vt, batch, seq_len)
    ys = _ssm(uc, batch, seq_len, w)
    x1, h2 = _mix_out(ys, u, a, x2d, w)
    return _ffn(h2, x1, w).reshape(batch, seq_len, d)


def kernel(x_prompt, x_sample, g_pre_mix, w_in, g_q_a, w_q_b, g_kv_a, w_kv_b, ssm_lam_re, ssm_lam_im, ssm_log_dt, ssm_b_re, ssm_b_im, ssm_c_re, ssm_c_im, ssm_d, w_glu, b_glu, g_out_attn, g_out_ssm, w_out, g_post_mix, g_pre_ffn, w_gate, w_up, w_down, g_post_ffn):
    weights = (g_pre_mix, w_in, g_q_a, w_q_b, g_kv_a, w_kv_b, ssm_lam_re, ssm_lam_im, ssm_log_dt,
               ssm_b_re, ssm_b_im, ssm_c_re, ssm_c_im, ssm_d, w_glu, b_glu, g_out_attn, g_out_ssm,
               w_out, g_post_mix, g_pre_ffn, w_gate, w_up, w_down, g_post_ffn)
    depth = g_pre_mix.shape[0]
    for layer in range(depth):
        w = _prepare_weights(*[p[layer] for p in weights])
        x_prompt = _layer(x_prompt, w)
        x_sample = _layer(x_sample, w)
    return (x_prompt, x_sample)
```
